```python
import math
import jax
import jax.numpy as jnp
from jax import lax
import numpy as np

D_MODEL = 4096
BATCH = 4
SEQ = 2048
DEPTH = 4
DEC_BATCH = 8
DEC_SEQ = 8
PAST_LEN = 8192
PAGE_SIZE = 128

H_RET = D_MODEL // 512
DK_RET = 256
DV_RET = 256
RET_W = H_RET * DV_RET
RET_CHUNK = 128
ROPE_BASE = 10000.0
CONV_W = D_MODEL // 2
CONV_K = 31
IN_EVEN = 2 * H_RET * DK_RET + 2 * RET_W + 2 * CONV_W
MIX_EVEN = RET_W + CONV_W
H_SB = 32
D_SB = D_MODEL // H_SB
SB_W = H_SB * D_SB
SB_BLOCK = 128
SB_BIAS_INIT = -6.0
D_FF = 11008
N_EXPERTS = 8
TOP_K = 2
D_EXPERT = 7168
D_PLE = 256
N_EVEN = (DEPTH + 1) // 2
N_ODD = DEPTH // 2
ALPHA = (2.0 * DEPTH) ** 0.25
BETA = (8.0 * DEPTH) ** -0.25
LN_EPS = 1e-5

kernel_name = 'hybrid_retention_conformer_stickbreaking_step'


def layer_norm(x, g, b):
    xf = x.astype(jnp.float32)
    mu = jnp.mean(xf, axis=-1, keepdims=True)
    xc = xf - mu
    var = jnp.mean(xc * xc, axis=-1, keepdims=True)
    return xc * lax.rsqrt(var + LN_EPS) * g.astype(jnp.float32) + b.astype(jnp.float32)


def post_norm(x, sub, g, b):
    return layer_norm(ALPHA * x + sub, g, b).astype(x.dtype)


def swiglu(x, w1, w3, w2):
    return (jax.nn.silu(x @ w1) * (x @ w3)) @ w2


def per_layer_embed(x, p_i, w_proj, w_gate):
    return x + (p_i.astype(x.dtype) @ w_proj) * jax.nn.sigmoid(x @ w_gate)


def rotary(x, pos):
    half = x.shape[-1] // 2
    inv = jnp.exp(-math.log(ROPE_BASE) * jnp.arange(half, dtype=jnp.float32) / half)
    ang = pos.astype(jnp.float32)[:, None] * inv[None, :]
    cos = jnp.cos(ang)[:, None, :]
    sin = jnp.sin(ang)[:, None, :]
    xf = x.astype(jnp.float32)
    x1, x2 = xf[..., :half], xf[..., half:]
    return jnp.concatenate([x1 * cos - x2 * sin, x1 * sin + x2 * cos], axis=-1)


def retention(q, k, v, r0, chunk, log_gamma):
    b, t, h, _ = q.shape
    dv = v.shape[-1]
    nc = t // chunk

    def to_chunks(a):
        return a.astype(jnp.float32).reshape(b, nc, chunk, h, a.shape[-1]).transpose(1, 0, 3, 2, 4)

    idx = jnp.arange(chunk, dtype=jnp.float32)
    lg = log_gamma[:, None]
    diff = idx[:, None] - idx[None, :]
    decay = jnp.where(diff >= 0, jnp.exp(lg[:, :, None] * jnp.maximum(diff, 0.0)), 0.0)
    q_scale = jnp.exp(lg * (idx + 1.0))[None, :, :, None]
    k_scale = jnp.exp(lg * (chunk - 1.0 - idx))[None, :, :, None]
    r_scale = jnp.exp(log_gamma * chunk)[None, :, None, None]

    def step(r, qkv):
        qc, kc, vc = qkv
        s = jnp.einsum('bhid,bhjd->bhij', qc, kc) * decay[None]
        inner = jnp.einsum('bhij,bhjv->bhiv', s, vc)
        cross = jnp.einsum('bhid,bhdv->bhiv', qc, r) * q_scale
        r_new = r * r_scale + jnp.einsum('bhjd,bhjv->bhdv', kc * k_scale, vc)
        return r_new, inner + cross

    r_fin, out = lax.scan(step, r0.astype(jnp.float32), (to_chunks(q), to_chunks(k), to_chunks(v)))
    out = out.transpose(1, 0, 3, 2, 4).reshape(b, t, h, dv)
    return out, r_fin


def causal_depthwise_conv(u, buf, w, bias):
    c = u.shape[-1]
    full = jnp.concatenate([buf.astype(u.dtype), u], axis=1)
    out = lax.conv_general_dilated(full, w.astype(u.dtype)[:, None, :], window_strides=(1,), padding='VALID',
                                   dimension_numbers=('NWC', 'WIO', 'NWC'), feature_group_count=c)
    return out + bias.astype(u.dtype), full[:, full.shape[1] - (CONV_K - 1):]


def stick_breaking(q, k, v, q_pos, k_pos, bias):
    b, t, h, d = q.shape
    blk = min(SB_BLOCK, t)
    nb = t // blk
    qb = q.reshape(b, nb, blk, h, d).transpose(1, 0, 2, 3, 4)
    qpb = q_pos.reshape(nb, blk)
    kf = k.astype(jnp.float32)
    vf = v.astype(jnp.float32)
    bf = bias.astype(jnp.float32)[None, :, None, None]
    scale = d ** -0.5

    def block(args):
        qi, qp = args
        z = jnp.einsum('bqhd,bkhd->bhqk', qi.astype(jnp.float32), kf) * scale + bf
        mask = (k_pos[None, :] < qp[:, None])[None, None]
        u = jnp.where(mask, jax.nn.log_sigmoid(-z), 0.0)
        tail = lax.cumsum(u, axis=3, reverse=True) - u
        a = jnp.where(mask, jnp.exp(jax.nn.log_sigmoid(z) + tail), 0.0)
        return jnp.einsum('bhqk,bkhd->bqhd', a, vf)

    o = lax.map(block, (qb, qpb))
    return o.transpose(1, 0, 2, 3, 4).reshape(b, t, h, d)


def moe_ffn(x, router, w1, w3, w2):
    b, t, d = x.shape
    xt = x.reshape(b * t, d)
    logits = (xt @ router).astype(jnp.float32)
    vals, idx = lax.top_k(logits, TOP_K)
    gates = jax.nn.softmax(vals, axis=-1)
    combine = jnp.sum(jax.nn.one_hot(idx, N_EXPERTS, dtype=jnp.float32) * gates[..., None], axis=1)
    out = jnp.zeros_like(xt)
    for e in range(N_EXPERTS):
        out = out + combine[:, e:e + 1].astype(xt.dtype) * swiglu(xt, w1[e], w3[e], w2[e])
    return out.reshape(b, t, d)


def even_layer(x, p_i, pos, r0, buf0, w):
    b, t, _ = x.shape
    h = x @ w['w_in']
    o1 = H_RET * DK_RET
    q = h[..., :o1]
    k = h[..., o1:2 * o1]
    v = h[..., 2 * o1:2 * o1 + RET_W]
    g = h[..., 2 * o1 + RET_W:2 * o1 + 2 * RET_W]
    ga = h[..., 2 * o1 + 2 * RET_W:2 * o1 + 2 * RET_W + CONV_W]
    gb = h[..., 2 * o1 + 2 * RET_W + CONV_W:]
    q = rotary(q.reshape(b, t, H_RET, DK_RET), pos)
    k = rotary(k.reshape(b, t, H_RET, DK_RET), pos) * (DK_RET ** -0.5)
    v = v.reshape(b, t, H_RET, DV_RET)
    log_gamma = jnp.log1p(-jnp.exp2(-5.0 - jnp.arange(H_RET, dtype=jnp.float32)))
    ret, r_new = retention(q, k, v, r0, min(RET_CHUNK, t), log_gamma)
    ret = layer_norm(ret, w['gn_g'].reshape(H_RET, DV_RET), w['gn_b'].reshape(H_RET, DV_RET))
    ret = jax.nn.silu(g) * ret.reshape(b, t, RET_W).astype(x.dtype)
    u = ga * jax.nn.sigmoid(gb)
    c, buf_new = causal_depthwise_conv(u, buf0, w['conv_w'], w['conv_b'])
    c = jax.nn.silu(layer_norm(c, w['cln_g'], w['cln_b']).astype(x.dtype))
    mix = jnp.concatenate([ret, c], axis=-1) @ w['w_out']
    x = post_norm(x, mix, w['ln1_g'], w['ln1_b'])
    x = post_norm(x, swiglu(x, w['ffn_w1'], w['ffn_w3'], w['ffn_w2']), w['ln2_g'], w['ln2_b'])
    x = per_layer_embed(x, p_i, w['ple_proj'], w['ple_gate'])
    return x, r_new, buf_new


def odd_layer(x, p_i, q_pos, k_pos, k_past, v_past, w):
    b, t, _ = x.shape
    h = x @ w['w_qkv']
    q = h[..., :SB_W].reshape(b, t, H_SB, D_SB)
    k = h[..., SB_W:2 * SB_W].reshape(b, t, H_SB, D_SB)
    v = h[..., 2 * SB_W:].reshape(b, t, H_SB, D_SB)
    k_all = jnp.concatenate([k_past.astype(k.dtype), k], axis=1)
    v_all = jnp.concatenate([v_past.astype(v.dtype), v], axis=1)
    o = stick_breaking(q, k_all, v_all, q_pos, k_pos, w['sb_bias']).reshape(b, t, SB_W).astype(x.dtype)
    x = post_norm(x, o @ w['w_out'], w['ln1_g'], w['ln1_b'])
    x = post_norm(x, moe_ffn(x, w['router'], w['moe_w1'], w['moe_w3'], w['moe_w2']), w['ln2_g'], w['ln2_b'])
    x = per_layer_embed(x, p_i, w['ple_proj'], w['ple_gate'])
    return x, k, v


def setup_inputs(seed: int = 0) -> dict:
    key = jax.random.key(seed)
    keys = list(jax.random.split(key, 40))

    def nrm(shape, scale):
        return jax.random.normal(keys.pop(), shape, jnp.float32) * scale

    def gain(shape):
        return 1.0 + nrm(shape, 0.01)

    n_pages = PAST_LEN // PAGE_SIZE
    n_used = DEC_BATCH * n_pages
    n_pool = n_used + max(1, n_used // 4)
    page_table = jax.random.permutation(keys.pop(), n_pool)[:n_used].reshape(DEC_BATCH, n_pages).astype(jnp.int32)

    ev_col = jnp.concatenate([jnp.ones((2 * H_RET * DK_RET,), jnp.float32), jnp.full((RET_W,), BETA, jnp.float32),
                              jnp.ones((RET_W,), jnp.float32), jnp.full((CONV_W,), BETA, jnp.float32),
                              jnp.ones((CONV_W,), jnp.float32)])
    sb_col = jnp.concatenate([jnp.ones((2 * SB_W,), jnp.float32), jnp.full((SB_W,), BETA, jnp.float32)])
    dm = D_MODEL ** -0.5
    return {
        'x_prompt': nrm((BATCH, SEQ, D_MODEL), 1.0),
        'x_sample': nrm((DEC_BATCH, DEC_SEQ, D_MODEL), 1.0),
        'p_prompt': nrm((DEPTH, BATCH, SEQ, D_PLE), 1.0),
        'p_sample': nrm((DEPTH, DEC_BATCH, DEC_SEQ, D_PLE), 1.0),
        'state_ret': nrm((N_EVEN, DEC_BATCH, H_RET, DK_RET, DV_RET), 0.5),
        'state_conv': nrm((N_EVEN, DEC_BATCH, CONV_K - 1, CONV_W), 0.5),
        'cache_k': nrm((N_ODD, n_pool, PAGE_SIZE, H_SB, D_SB), 1.0),
        'cache_v': nrm((N_ODD, n_pool, PAGE_SIZE, H_SB, D_SB), 1.0),
        'page_table': page_table,
        'ln_mix_g': gain((DEPTH, D_MODEL)),
        'ln_mix_b': nrm((DEPTH, D_MODEL), 0.01),
        'ln_ffn_g': gain((DEPTH, D_MODEL)),
        'ln_ffn_b': nrm((DEPTH, D_MODEL), 0.01),
        'ple_proj': nrm((DEPTH, D_PLE, D_MODEL), D_PLE ** -0.5),
        'ple_gate': nrm((DEPTH, D_MODEL, D_MODEL), dm),
        'ev_w_in': nrm((N_EVEN, D_MODEL, IN_EVEN), dm) * ev_col,
        'ev_w_out': nrm((N_EVEN, MIX_EVEN, D_MODEL), MIX_EVEN ** -0.5 * BETA),
        'ret_gn_g': gain((N_EVEN, RET_W)),
        'ret_gn_b': nrm((N_EVEN, RET_W), 0.01),
        'conv_w': nrm((N_EVEN, CONV_K, CONV_W), CONV_K ** -0.5),
        'conv_b': nrm((N_EVEN, CONV_W), 0.01),
        'conv_ln_g': gain((N_EVEN, CONV_W)),
        'conv_ln_b': nrm((N_EVEN, CONV_W), 0.01),
        'ffn_w1': nrm((N_EVEN, D_MODEL, D_FF), dm),
        'ffn_w3': nrm((N_EVEN, D_MODEL, D_FF), dm),
        'ffn_w2': nrm((N_EVEN, D_FF, D_MODEL), D_FF ** -0.5 * BETA),
        'sb_w_qkv': nrm((N_ODD, D_MODEL, 3 * SB_W), dm) * sb_col,
        'sb_w_out': nrm((N_ODD, SB_W, D_MODEL), SB_W ** -0.5 * BETA),
        'sb_bias': SB_BIAS_INIT + nrm((N_ODD, H_SB), 0.1),
        'moe_router': nrm((N_ODD, D_MODEL, N_EXPERTS), dm),
        'moe_w1': nrm((N_ODD, N_EXPERTS, D_MODEL, D_EXPERT), dm),
        'moe_w3': nrm((N_ODD, N_EXPERTS, D_MODEL, D_EXPERT), dm),
        'moe_w2': nrm((N_ODD, N_EXPERTS, D_EXPERT, D_MODEL), D_EXPERT ** -0.5 * BETA),
    }


def reference(x_prompt, x_sample, p_prompt, p_sample, state_ret, state_conv, cache_k, cache_v, page_table,
              ln_mix_g, ln_mix_b, ln_ffn_g, ln_ffn_b, ple_proj, ple_gate, ev_w_in, ev_w_out, ret_gn_g, ret_gn_b,
              conv_w, conv_b, conv_ln_g, conv_ln_b, ffn_w1, ffn_w3, ffn_w2, sb_w_qkv, sb_w_out, sb_bias, moe_router,
              moe_w1, moe_w3, moe_w2):
    pos_p = jnp.arange(SEQ, dtype=jnp.int32)
    pos_s = PAST_LEN + jnp.arange(DEC_SEQ, dtype=jnp.int32)
    kpos_s = jnp.arange(PAST_LEN + DEC_SEQ, dtype=jnp.int32)
    xp, xs = x_prompt, x_sample
    ret_p, ret_s, conv_p, conv_s = [], [], [], []
    kp, vp, ks, vs = [], [], [], []
    for i in range(DEPTH):
        shared = dict(ln1_g=ln_mix_g[i], ln1_b=ln_mix_b[i], ln2_g=ln_ffn_g[i], ln2_b=ln_ffn_b[i],
                      ple_proj=ple_proj[i], ple_gate=ple_gate[i])
        if i % 2 == 0:
            e = i // 2
            w = dict(shared, w_in=ev_w_in[e], w_out=ev_w_out[e], gn_g=ret_gn_g[e], gn_b=ret_gn_b[e],
                     conv_w=conv_w[e], conv_b=conv_b[e], cln_g=conv_ln_g[e], cln_b=conv_ln_b[e],
                     ffn_w1=ffn_w1[e], ffn_w3=ffn_w3[e], ffn_w2=ffn_w2[e])
            r0 = jnp.zeros((BATCH, H_RET, DK_RET, DV_RET), jnp.float32)
            buf0 = jnp.zeros((BATCH, CONV_K - 1, CONV_W), xp.dtype)
            xp, r_new, c_new = even_layer(xp, p_prompt[i], pos_p, r0, buf0, w)
            ret_p.append(r_new)
            conv_p.append(c_new)
            xs, r_new, c_new = even_layer(xs, p_sample[i], pos_s, state_ret[e], state_conv[e], w)
            ret_s.append(r_new)
            conv_s.append(c_new)
        else:
            o = i // 2
            w = dict(shared, w_qkv=sb_w_qkv[o], w_out=sb_w_out[o], sb_bias=sb_bias[o], router=moe_router[o],
                     moe_w1=moe_w1[o], moe_w3=moe_w3[o], moe_w2=moe_w2[o])
            empty = jnp.zeros((BATCH, 0, H_SB, D_SB), xp.dtype)
            xp, k_new, v_new = odd_layer(xp, p_prompt[i], pos_p, pos_p, empty, empty, w)
            kp.append(k_new)
            vp.append(v_new)
            k_past = cache_k[o][page_table].reshape(DEC_BATCH, -1, H_SB, D_SB)
            v_past = cache_v[o][page_table].reshape(DEC_BATCH, -1, H_SB, D_SB)
            xs, k_new, v_new = odd_layer(xs, p_sample[i], pos_s, kpos_s, k_past, v_past, w)
            ks.append(k_new)
            vs.append(v_new)
    return (xp, xs, jnp.stack(ret_p), jnp.stack(ret_s), jnp.stack(conv_p), jnp.stack(conv_s),
            jnp.stack(kp), jnp.stack(vp), jnp.stack(ks), jnp.stack(vs))
```

```python
import functools
import math

import jax
import jax.numpy as jnp
from jax import lax
from jax.experimental import pallas as pl
from jax.experimental.pallas import tpu as pltpu

F32 = jnp.float32
BF16 = jnp.bfloat16

DEPTH = 4
H_RET = 8
DK_RET = 256
DV_RET = 256
RET_W = H_RET * DV_RET
RET_CHUNK = 128
ROPE_BASE = 10000.0
CONV_K = 31
H_SB = 32
D_SB = 128
SB_W = H_SB * D_SB
TOP_K = 2
ALPHA = (2.0 * DEPTH) ** 0.25
LN_EPS = 1e-5

V7X_VMEM_LIMIT_BYTES = 56 * 1024 * 1024
LANES = 128
BF16_SUBLANES = 16

MOE_ROW_TILE = 256
SB_Q_TILE = 256
SB_PAGES_PER_STEP = 4
SB_Q_PAD = 16


def _cparams(*sem):
    return pltpu.CompilerParams(dimension_semantics=sem, vmem_limit_bytes=V7X_VMEM_LIMIT_BYTES)


def _div_tile(n, cap, mult):
    best = None
    for t in range(mult, min(n, cap) + 1, mult):
        if n % t == 0:
            best = t
    assert best is not None, (n, cap, mult)
    return best


def _sigmoid(x):
    return 1.0 / (1.0 + jnp.exp(-x))


def _silu(x):
    return x * _sigmoid(x)


def _layer_norm(y, g, b):
    mu = jnp.mean(y, axis=-1, keepdims=True)
    yc = y - mu
    var = jnp.mean(yc * yc, axis=-1, keepdims=True)
    return yc * lax.rsqrt(var + LN_EPS) * g + b


def _cast_panel(w_ref, wb_ref, kc):
    def body(c, carry):
        r = pl.multiple_of(c * kc, kc)
        wb_ref[pl.ds(r, kc), :] = w_ref[pl.ds(r, kc), :].astype(BF16)
        return carry
    lax.fori_loop(0, w_ref.shape[0] // kc, body, 0)


def _mm_kernel(gid_ref, nv_ref, x_ref, *rest, n_w, n_extra, epi, kc):
    w_refs = rest[:n_w]
    extras = rest[n_w:n_w + n_extra]
    outs = rest[n_w + n_extra:len(rest) - n_w]
    wbs = rest[len(rest) - n_w:]
    i = pl.program_id(1)
    changed = jnp.logical_or(i == 0, gid_ref[i] != gid_ref[jnp.maximum(i - 1, 0)])

    @pl.when(changed)
    def _():
        for w_ref, wb in zip(w_refs, wbs):
            _cast_panel(w_ref, wb, kc)

    @pl.when(i < nv_ref[0])
    def _():
        x = x_ref[...]
        accs = [jnp.dot(x, wb[...], preferred_element_type=F32) for wb in wbs]
        epi(accs, extras, outs)

    @pl.when(i >= nv_ref[0])
    def _():
        for o in outs:
            o[...] = jnp.zeros(o.shape, o.dtype)


def _epi_store(accs, extras, outs):
    outs[0][...] = accs[0].astype(outs[0].dtype)


def _epi_add(accs, extras, outs):
    outs[0][...] = accs[0] + extras[0][...]


def _epi_swiglu(accs, extras, outs):
    outs[0][...] = (_silu(accs[0]) * accs[1]).astype(outs[0].dtype)


def _epi_rowscale(accs, extras, outs):
    outs[0][...] = accs[0] * extras[0][...]


def _epi_ple(accs, extras, outs):
    xres, p_ref, wp_ref = extras
    pp = jnp.dot(p_ref[...], wp_ref[...].astype(BF16), preferred_element_type=F32)
    y = xres[...] + pp * _sigmoid(accs[0])
    outs[0][...] = y
    outs[1][...] = y.astype(BF16)


def _matmul(x, ws, gbase, *, name, tn, tm, epi, out_dtypes, extras=(), gid=None, n_valid=None, kpanel=None):
    M = x.shape[0]
    N = ws[0].shape[2]
    kblk, K = (0, x.shape[1]) if kpanel is None else kpanel
    assert all(w.shape[1] == x.shape[1] for w in ws) and x.shape[1] % K == 0
    assert M % tm == 0 and N % tn == 0
    n_i = M // tm
    if gid is None:
        gid = jnp.zeros((n_i,), jnp.int32)
        n_valid = jnp.full((1,), n_i, jnp.int32)
    kc = _div_tile(K, 1024, BF16_SUBLANES)
    in_specs = [pl.BlockSpec((tm, K), lambda j, i, g, nv: (i, kblk))]
    for _ in ws:
        in_specs.append(pl.BlockSpec((None, K, tn), lambda j, i, g, nv: (gbase + g[i], kblk, j)))
    for _, bs, im in extras:
        in_specs.append(pl.BlockSpec(bs, im))
    out_shape = [jax.ShapeDtypeStruct((M, N), dt) for dt in out_dtypes]
    out_specs = [pl.BlockSpec((tm, tn), lambda j, i, g, nv: (i, j)) for _ in out_dtypes]
    kern = functools.partial(_mm_kernel, n_w=len(ws), n_extra=len(extras), epi=epi, kc=kc)
    res = pl.pallas_call(
        kern,
        grid_spec=pltpu.PrefetchScalarGridSpec(
            num_scalar_prefetch=2, grid=(N // tn, n_i), in_specs=in_specs, out_specs=out_specs,
            scratch_shapes=[pltpu.VMEM((K, tn), BF16) for _ in ws]),
        out_shape=out_shape, name=name,
        compiler_params=_cparams("arbitrary", "arbitrary"),
    )(gid, n_valid, x, *ws, *[e[0] for e in extras])
    return res


def _post_norm_kernel(x_ref, s_ref, g_ref, b_ref, o_ref, ob_ref):
    y = _layer_norm(ALPHA * x_ref[...] + s_ref[...], g_ref[...], b_ref[...])
    o_ref[...] = y
    ob_ref[...] = y.astype(BF16)


def _post_norm(x, sub, g, b, layer):
    M, D = x.shape
    tm = _div_tile(M, 256, BF16_SUBLANES)
    g3 = g.reshape(g.shape[0], 1, D)
    b3 = b.reshape(b.shape[0], 1, D)
    row = pl.BlockSpec((tm, D), lambda i: (i, 0))
    vec = pl.BlockSpec((None, 1, D), lambda i: (layer, 0, 0))
    return pl.pallas_call(
        _post_norm_kernel, grid=(M // tm,), in_specs=[row, row, vec, vec], out_specs=[row, row],
        out_shape=[jax.ShapeDtypeStruct((M, D), F32), jax.ShapeDtypeStruct((M, D), BF16)],
        name="post_norm", compiler_params=_cparams("arbitrary"),
    )(x, sub, g3, b3)


def _norm_silu_kernel(c_ref, g_ref, b_ref, o_ref):
    o_ref[...] = _silu(_layer_norm(c_ref[...], g_ref[...], b_ref[...])).astype(BF16)


def _norm_silu(c, g, b, layer):
    M, D = c.shape
    tm = _div_tile(M, 512, BF16_SUBLANES)
    g3 = g.reshape(g.shape[0], 1, D)
    b3 = b.reshape(b.shape[0], 1, D)
    row = pl.BlockSpec((tm, D), lambda i: (i, 0))
    vec = pl.BlockSpec((None, 1, D), lambda i: (layer, 0, 0))
    return pl.pallas_call(
        _norm_silu_kernel, grid=(M // tm,), in_specs=[row, vec, vec], out_specs=row,
        out_shape=jax.ShapeDtypeStruct((M, D), BF16), name="conv_norm_silu",
        compiler_params=_cparams("arbitrary"),
    )(c, g3, b3)


def _retention_kernel(q_ref, k_ref, v_ref, g_ref, cos_ref, sin_ref, dec_ref, qs_ref, ks_ref, rs_ref,
                      gg_ref, gb_ref, r0_ref, o_ref, st_ref, r_scr, *, n_chunks):
    c = pl.program_id(2)

    @pl.when(c == 0)
    def _():
        r_scr[...] = r0_ref[...]

    cos = cos_ref[...]
    sin = sin_ref[...]
    half = DK_RET // 2

    def rot(x):
        x1 = x[:, :half]
        x2 = x[:, half:]
        return jnp.concatenate([x1 * cos - x2 * sin, x1 * sin + x2 * cos], axis=-1)

    qr = rot(q_ref[...])
    kr = rot(k_ref[...]) * (DK_RET ** -0.5)
    qb = qr.astype(BF16)
    kb = kr.astype(BF16)
    vb = v_ref[...].astype(BF16)
    s = lax.dot_general(qb, kb, (((1,), (1,)), ((), ())), preferred_element_type=F32) * dec_ref[...]
    inner = jnp.dot(s.astype(BF16), vb, preferred_element_type=F32)
    r = r_scr[...]
    cross = jnp.dot(qb, r.astype(BF16), preferred_element_type=F32) * qs_ref[...]
    ksc = (kr * ks_ref[...]).astype(BF16)
    r_new = r * rs_ref[...] + lax.dot_general(ksc, vb, (((0,), (0,)), ((), ())), preferred_element_type=F32)
    r_scr[...] = r_new
    y = _layer_norm(inner + cross, gg_ref[...], gb_ref[...])
    o_ref[...] = (_silu(g_ref[...]) * y).astype(o_ref.dtype)

    @pl.when(c == n_chunks - 1)
    def _():
        st_ref[...] = r_new


def _retention(h, row_blk0, n_batch, n_chunks, cos, sin, decay, qsc, ksc, rsc, gn_g, gn_b, r0, layer):
    C = RET_CHUNK
    hq = DK_RET
    ncb = RET_W // hq

    def rows(col0):
        return pl.BlockSpec((C, hq), lambda b, hh, c: (row_blk0 + b * n_chunks + c, col0 + hh))

    tab = pl.BlockSpec((C, DK_RET // 2), lambda b, hh, c: (c, 0))
    per_head = lambda shp: pl.BlockSpec((None,) + shp, lambda b, hh, c: (hh, 0, 0))
    gvec = pl.BlockSpec((None, 1, hq), lambda b, hh, c: (layer, 0, hh))
    state = pl.BlockSpec((None, None, DK_RET, DV_RET), lambda b, hh, c: (b, hh, 0, 0))
    out_rows = n_batch * n_chunks * C
    return pl.pallas_call(
        functools.partial(_retention_kernel, n_chunks=n_chunks),
        grid=(n_batch, H_RET, n_chunks),
        in_specs=[rows(0), rows(ncb), rows(2 * ncb), rows(3 * ncb), tab, tab,
                  per_head((C, C)), per_head((C, 1)), per_head((C, 1)), per_head((1, DV_RET)),
                  gvec, gvec, state],
        out_specs=[pl.BlockSpec((C, hq), lambda b, hh, c: (b * n_chunks + c, hh)), state],
        out_shape=[jax.ShapeDtypeStruct((out_rows, RET_W), BF16),
                   jax.ShapeDtypeStruct((n_batch, H_RET, DK_RET, DV_RET), F32)],
        scratch_shapes=[pltpu.VMEM((DK_RET, DV_RET), F32)], name="retention",
        compiler_params=_cparams("arbitrary", "arbitrary", "arbitrary"),
    )(h, h, h, h, cos, sin, decay, qsc, ksc, rsc,
      gn_g.reshape(gn_g.shape[0], 1, RET_W), gn_b.reshape(gn_b.shape[0], 1, RET_W), r0)


def _retention_tables(chunk, pad_to):
    log_gamma = jnp.log1p(-jnp.exp2(-5.0 - jnp.arange(H_RET, dtype=F32)))
    idx = jnp.arange(chunk, dtype=F32)
    lg = log_gamma[:, None]
    diff = idx[:, None] - idx[None, :]
    decay = jnp.where(diff >= 0, jnp.exp(lg[:, :, None] * jnp.maximum(diff, 0.0)), 0.0)
    qsc = jnp.exp(lg * (idx + 1.0))[:, :, None]
    ksc = jnp.exp(lg * (chunk - 1.0 - idx))[:, :, None]
    rsc = jnp.broadcast_to(jnp.exp(log_gamma * chunk)[:, None, None], (H_RET, 1, DV_RET))
    p = pad_to - chunk
    decay = jnp.pad(decay, ((0, 0), (0, p), (0, p)))
    qsc = jnp.pad(qsc, ((0, 0), (0, p), (0, 0)))
    ksc = jnp.pad(ksc, ((0, 0), (0, p), (0, 0)))
    return decay, qsc, ksc, rsc


def _rope_tables(pos, pad_to):
    half = DK_RET // 2
    inv = jnp.exp(-math.log(ROPE_BASE) * jnp.arange(half, dtype=F32) / half)
    ang = pos.astype(F32)[:, None] * inv[None, :]
    p = pad_to - pos.shape[0]
    return jnp.pad(jnp.cos(ang), ((0, p), (0, 0))), jnp.pad(jnp.sin(ang), ((0, p), (0, 0)))


CONV_HIST = 32
CONV_LANES = 512
CONV_ROWS = 64


def _conv_kernel(a_ref, b_ref, st_ref, w_ref, bias_ref, o_ref, buf_ref, full, *, tt, n_t):
    t = pl.program_id(2)
    kh = CONV_K - 1
    off = CONV_HIST - kh

    @pl.when(t == 0)
    def _():
        full[0:off, :] = jnp.zeros((off, CONV_LANES), F32)
        full[off:CONV_HIST, :] = st_ref[...]

    @pl.when(t > 0)
    def _():
        full[0:CONV_HIST, :] = full[tt:tt + CONV_HIST, :]

    full[CONV_HIST:CONV_HIST + tt, :] = a_ref[...] * _sigmoid(b_ref[...])
    rc = min(CONV_ROWS, tt)
    for r0 in range(0, tt, rc):
        acc = jnp.zeros((rc, CONV_LANES), F32) + bias_ref[...]
        for j in range(CONV_K):
            acc = acc + w_ref[j:j + 1, :] * full[off + r0 + j:off + r0 + j + rc, :]
        o_ref[r0:r0 + rc, :] = acc

    @pl.when(t == n_t - 1)
    def _():
        buf_ref[...] = full[CONV_HIST + tt - kh:CONV_HIST + tt, :]


def _conv(h, row_blk0, n_batch, n_t, tt, col_a, col_b, state, w, bias, layer):
    W = w.shape[2]
    n_l = W // CONV_LANES
    kh = CONV_K - 1
    rows = lambda col0: pl.BlockSpec((tt, CONV_LANES), lambda b, l, t: (row_blk0 + b * n_t + t, col0 + l))
    return pl.pallas_call(
        functools.partial(_conv_kernel, tt=tt, n_t=n_t),
        grid=(n_batch, n_l, n_t),
        in_specs=[rows(col_a), rows(col_b),
                  pl.BlockSpec((None, kh, CONV_LANES), lambda b, l, t: (b, 0, l)),
                  pl.BlockSpec((None, CONV_K, CONV_LANES), lambda b, l, t: (layer, 0, l)),
                  pl.BlockSpec((None, 1, CONV_LANES), lambda b, l, t: (layer, 0, l))],
        out_specs=[pl.BlockSpec((tt, CONV_LANES), lambda b, l, t: (b * n_t + t, l)),
                   pl.BlockSpec((None, kh, CONV_LANES), lambda b, l, t: (b, 0, l))],
        out_shape=[jax.ShapeDtypeStruct((n_batch * n_t * tt, W), F32),
                   jax.ShapeDtypeStruct((n_batch, kh, W), F32)],
        scratch_shapes=[pltpu.VMEM((CONV_HIST + tt, CONV_LANES), F32)], name="glu_conv",
        compiler_params=_cparams("arbitrary", "arbitrary", "arbitrary"),
    )(h, h, state, w, bias.reshape(bias.shape[0], 1, W))


def _sb_block(z, mask, carry, tri):
    sp = jnp.log1p(jnp.exp(-jnp.abs(z)))
    u = -jnp.maximum(z, 0.0) - sp
    lb = jnp.minimum(z, 0.0) - sp
    if mask is not None:
        u = jnp.where(mask, u, 0.0)
    u_hi = u.astype(BF16)
    u_lo = (u - u_hi.astype(F32)).astype(BF16)
    tail = jnp.dot(u_hi, tri, preferred_element_type=F32) + jnp.dot(u_lo, tri, preferred_element_type=F32)
    a = jnp.exp(lb + tail + carry)
    if mask is not None:
        a = jnp.where(mask, a, 0.0)
    return a, carry + tail[:, :1] + u[:, :1]


def _tri(n):
    r = jnp.arange(n)
    return (r[:, None] > r[None, :]).astype(BF16)


def _sb_prompt_kernel(bias_ref, q_ref, k_ref, v_ref, tri_ref, o_ref, *, tq):
    hh = pl.program_id(1)
    qi = pl.program_id(2)
    qb = q_ref[...].astype(BF16)
    bias = bias_ref[hh]
    scale = D_SB ** -0.5
    tri = tri_ref[...]
    row = lax.broadcasted_iota(jnp.int32, (tq, tq), 0)
    col = lax.broadcasted_iota(jnp.int32, (tq, tq), 1)
    causal = col < row

    def step(kk, state):
        carry, o = state
        r0 = pl.multiple_of((qi - kk) * tq, tq)
        kb = k_ref[pl.ds(r0, tq), :].astype(BF16)
        vb = v_ref[pl.ds(r0, tq), :].astype(BF16)
        z = lax.dot_general(qb, kb, (((1,), (1,)), ((), ())), preferred_element_type=F32) * scale + bias
        mask = jnp.logical_or(causal, kk > 0)
        a, carry = _sb_block(z, mask, carry, tri)
        return carry, o + jnp.dot(a.astype(BF16), vb, preferred_element_type=F32)

    _, o = lax.fori_loop(0, qi + 1, step, (jnp.zeros((tq, 1), F32), jnp.zeros((tq, D_SB), F32)))
    o_ref[...] = o.astype(BF16)


def _sb_prompt(h, bias, n_batch, seq):
    tq = min(SB_Q_TILE, seq)
    nq = seq // tq
    return pl.pallas_call(
        functools.partial(_sb_prompt_kernel, tq=tq),
        grid_spec=pltpu.PrefetchScalarGridSpec(
            num_scalar_prefetch=1, grid=(n_batch, H_SB, nq),
            in_specs=[pl.BlockSpec((tq, D_SB), lambda b, hh, qi, bias: (b * nq + qi, hh)),
                      pl.BlockSpec((seq, D_SB), lambda b, hh, qi, bias: (b, H_SB + hh)),
                      pl.BlockSpec((seq, D_SB), lambda b, hh, qi, bias: (b, 2 * H_SB + hh)),
                      pl.BlockSpec((tq, tq), lambda b, hh, qi, bias: (0, 0))],
            out_specs=pl.BlockSpec((tq, D_SB), lambda b, hh, qi, bias: (b * nq + qi, hh))),
        out_shape=jax.ShapeDtypeStruct((n_batch * seq, SB_W), BF16), name="sb_prompt",
        compiler_params=_cparams("arbitrary", "arbitrary", "arbitrary"),
    )(bias, h, h, h, _tri(tq))


def _sb_sample_kernel(pt_ref, bias_ref, q_ref, kn_ref, vn_ref, *rest, n_steps, t_dec, page):
    P = SB_PAGES_PER_STEP
    k_pages = rest[:P]
    v_pages = rest[P:2 * P]
    tri_ref, o_ref, qs, carry, acc = rest[2 * P:]
    s = pl.program_id(1)
    scale = D_SB ** -0.5
    QP = SB_Q_PAD
    rows = H_SB * QP

    def bias_col():
        hrow = lax.broadcasted_iota(jnp.int32, (rows, 1), 0) // QP
        out = jnp.zeros((rows, 1), F32)
        for hh in range(H_SB):
            out = jnp.where(hrow == hh, bias_ref[hh], out)
        return out

    def process(kcat, vcat, nk, mask, tri):
        z = jnp.concatenate(
            [lax.dot_general(qs[hh], kcat[hh], (((1,), (1,)), ((), ())), preferred_element_type=F32)
             for hh in range(H_SB)], axis=0)
        z = z * scale + bias_col()
        a, new_carry = _sb_block(z, mask, carry[...], tri)
        carry[...] = new_carry
        ab = a.astype(BF16)
        for hh in range(H_SB):
            acc[hh * QP:(hh + 1) * QP, :] += jnp.dot(ab[hh * QP:(hh + 1) * QP, :], vcat[hh],
                                                     preferred_element_type=F32)

    @pl.when(s == 0)
    def _():
        carry[...] = jnp.zeros_like(carry)
        acc[...] = jnp.zeros_like(acc)
        pad = jnp.zeros((QP - t_dec, D_SB), F32)
        for hh in range(H_SB):
            qs[hh] = jnp.concatenate([q_ref[:, hh * D_SB:(hh + 1) * D_SB], pad], axis=0).astype(BF16)
        kpad = jnp.zeros((page - t_dec, D_SB), F32)
        kcat = [jnp.concatenate([kn_ref[:, hh * D_SB:(hh + 1) * D_SB], kpad], axis=0).astype(BF16)
                for hh in range(H_SB)]
        vcat = [jnp.concatenate([vn_ref[:, hh * D_SB:(hh + 1) * D_SB], kpad], axis=0).astype(BF16)
                for hh in range(H_SB)]
        trow = lax.broadcasted_iota(jnp.int32, (rows, page), 0) % QP
        col = lax.broadcasted_iota(jnp.int32, (rows, page), 1)
        process(kcat, vcat, page, col < trow, tri_ref[0:page, 0:page])

    @pl.when(s > 0)
    def _():
        kcat = [jnp.concatenate([kp[:, hh * D_SB:(hh + 1) * D_SB] for kp in k_pages], axis=0).astype(BF16)
                for hh in range(H_SB)]
        vcat = [jnp.concatenate([vp[:, hh * D_SB:(hh + 1) * D_SB] for vp in v_pages], axis=0).astype(BF16)
                for hh in range(H_SB)]
        process(kcat, vcat, P * page, None, tri_ref[...])

    @pl.when(s == n_steps - 1)
    def _():
        for hh in range(H_SB):
            o_ref[:, hh * D_SB:(hh + 1) * D_SB] = acc[hh * QP:hh * QP + t_dec, :]


def _sb_sample(h, row_blk0, cache_k, cache_v, page_table, bias, layer, n_batch, t_dec):
    P = SB_PAGES_PER_STEP
    n_pages = page_table.shape[1]
    page = cache_k.shape[2]
    assert n_pages % P == 0 and t_dec <= SB_Q_PAD and t_dec <= page
    n_groups = n_pages // P
    n_steps = n_groups + 1
    ck = cache_k.reshape(cache_k.shape[0], cache_k.shape[1], page, SB_W)
    cv = cache_v.reshape(cache_v.shape[0], cache_v.shape[1], page, SB_W)

    def page_spec(i):
        def im(b, s, pt, bias):
            g = n_groups - jnp.maximum(s, 1)
            return (layer, pt[b * n_pages + g * P + i], 0, 0)
        return pl.BlockSpec((None, None, page, SB_W), im)

    new = lambda col: pl.BlockSpec((t_dec, SB_W), lambda b, s, pt, bias: (row_blk0 + b, col))
    return pl.pallas_call(
        functools.partial(_sb_sample_kernel, n_steps=n_steps, t_dec=t_dec, page=page),
        grid_spec=pltpu.PrefetchScalarGridSpec(
            num_scalar_prefetch=2, grid=(n_batch, n_steps),
            in_specs=[new(0), new(1), new(2)] + [page_spec(i) for i in range(P)] * 2
            + [pl.BlockSpec((P * page, P * page), lambda b, s, pt, bias: (0, 0))],
            out_specs=pl.BlockSpec((t_dec, SB_W), lambda b, s, pt, bias: (b, 0)),
            scratch_shapes=[pltpu.VMEM((H_SB, SB_Q_PAD, D_SB), BF16),
                            pltpu.VMEM((H_SB * SB_Q_PAD, 1), F32),
                            pltpu.VMEM((H_SB * SB_Q_PAD, D_SB), F32)]),
        out_shape=jax.ShapeDtypeStruct((n_batch * t_dec, SB_W), F32), name="sb_sample",
        compiler_params=_cparams("arbitrary", "arbitrary"),
    )(page_table.reshape(-1), bias, h, h, h, *([ck] * P), *([cv] * P), _tri(P * page))


def _split_bf16(x):
    hi = x.astype(BF16)
    return hi, (x - hi.astype(F32)).astype(BF16)


def _router_kernel(x_ref, w_ref, g_ref, e_ref, *, n_exp):
    xh, xl = _split_bf16(x_ref[...])
    wh, wl = _split_bf16(w_ref[...])
    dot = functools.partial(jnp.dot, preferred_element_type=F32)
    logits = dot(xh, wh) + dot(xl, wh) + dot(xh, wl)
    lane = lax.broadcasted_iota(jnp.int32, logits.shape, 1)
    lanef = lane.astype(F32)
    neg = jnp.float32(-jnp.inf)
    logits = jnp.where(lane < n_exp, logits, neg)
    m1 = jnp.max(logits, axis=-1, keepdims=True)
    i1 = jnp.min(jnp.where(logits == m1, lanef, float(LANES)), axis=-1, keepdims=True)
    rest = jnp.where(lanef == i1, neg, logits)
    m2 = jnp.max(rest, axis=-1, keepdims=True)
    i2 = jnp.min(jnp.where(rest == m2, lanef, float(LANES)), axis=-1, keepdims=True)
    e2 = jnp.exp(m2 - m1)
    den = 1.0 + e2
    g_ref[...] = jnp.where(lane == 0, 1.0 / den, jnp.where(lane == 1, e2 / den, 0.0))
    e_ref[...] = jnp.where(lane == 0, i1, jnp.where(lane == 1, i2, 0.0)).astype(jnp.int32)


def _router(x, router_w, layer):
    M, D = x.shape
    n_exp = router_w.shape[2]
    tm = _div_tile(M, 256, 8)
    wpad = jnp.pad(router_w, ((0, 0), (0, 0), (0, LANES - n_exp)))
    row = lambda w: pl.BlockSpec((tm, w), lambda i: (i, 0))
    gates, eidx = pl.pallas_call(
        functools.partial(_router_kernel, n_exp=n_exp), grid=(M // tm,),
        in_specs=[row(D), pl.BlockSpec((None, D, LANES), lambda i: (layer, 0, 0))],
        out_specs=[row(LANES), row(LANES)],
        out_shape=[jax.ShapeDtypeStruct((M, LANES), F32), jax.ShapeDtypeStruct((M, LANES), jnp.int32)],
        name="moe_router", compiler_params=_cparams("arbitrary"),
    )(x, wpad)
    return gates[:, :TOP_K], eidx[:, :TOP_K]


def _row_copy(src_hbm, dst, src_row, dst_row, sem):
    return pltpu.make_async_copy(src_hbm.at[pl.ds(src_row, 1), :], dst.at[pl.ds(dst_row, 1), :], sem)


def _gather_kernel(tok_ref, x_hbm, o_ref, buf, sem, *, tg):
    i = pl.program_id(0)

    def issue(r, carry):
        _row_copy(x_hbm, buf, tok_ref[i * tg + r], r, sem).start()
        return carry

    def wait(r, carry):
        _row_copy(x_hbm, buf, 0, r, sem).wait()
        return carry

    lax.fori_loop(0, tg, issue, 0)
    lax.fori_loop(0, tg, wait, 0)
    o_ref[...] = buf[...].astype(BF16)


def _gather_rows(x, row_token, tg):
    R = row_token.shape[0]
    D = x.shape[1]
    return pl.pallas_call(
        functools.partial(_gather_kernel, tg=tg),
        grid_spec=pltpu.PrefetchScalarGridSpec(
            num_scalar_prefetch=1, grid=(R // tg,),
            in_specs=[pl.BlockSpec(memory_space=pl.ANY)],
            out_specs=pl.BlockSpec((tg, D), lambda i, tok: (i, 0)),
            scratch_shapes=[pltpu.VMEM((tg, D), F32), pltpu.SemaphoreType.DMA(())]),
        out_shape=jax.ShapeDtypeStruct((R, D), BF16), name="moe_gather",
        compiler_params=_cparams("arbitrary"),
    )(row_token, x)


def _combine_norm_kernel(dest_ref, x_ref, g_ref, b_ref, y_hbm, o_ref, ob_ref, buf, sem, *, tc):
    i = pl.program_id(0)

    def issue(r, carry):
        for k in range(TOP_K):
            _row_copy(y_hbm, buf.at[k], dest_ref[(i * tc + r) * TOP_K + k], r, sem).start()
        return carry

    def wait(r, carry):
        for k in range(TOP_K):
            _row_copy(y_hbm, buf.at[k], 0, r, sem).wait()
        return carry

    lax.fori_loop(0, tc, issue, 0)
    lax.fori_loop(0, tc, wait, 0)
    moe = buf[0] + buf[1]
    y = _layer_norm(ALPHA * x_ref[...] + moe, g_ref[...], b_ref[...])
    o_ref[...] = y
    ob_ref[...] = y.astype(BF16)


def _combine_norm(x, y_rows, dest, g, b, layer):
    M, D = x.shape
    tc = _div_tile(M, 64, BF16_SUBLANES)
    g3 = g.reshape(g.shape[0], 1, D)
    b3 = b.reshape(b.shape[0], 1, D)
    row = pl.BlockSpec((tc, D), lambda i, d: (i, 0))
    vec = pl.BlockSpec((None, 1, D), lambda i, d: (layer, 0, 0))
    return pl.pallas_call(
        functools.partial(_combine_norm_kernel, tc=tc),
        grid_spec=pltpu.PrefetchScalarGridSpec(
            num_scalar_prefetch=1, grid=(M // tc,),
            in_specs=[row, vec, vec, pl.BlockSpec(memory_space=pl.ANY)],
            out_specs=[row, row],
            scratch_shapes=[pltpu.VMEM((TOP_K, tc, D), F32), pltpu.SemaphoreType.DMA(())]),
        out_shape=[jax.ShapeDtypeStruct((M, D), F32), jax.ShapeDtypeStruct((M, D), BF16)],
        name="moe_combine_norm", compiler_params=_cparams("arbitrary"),
    )(dest.reshape(-1), x, g3, b3, y_rows)


def _moe_plan(eidx, gates, n_exp, tile):
    M = eidx.shape[0]
    n_tiles = (M * TOP_K + n_exp * (tile - 1)) // tile
    R = n_tiles * tile
    onehot = (eidx[:, :, None] == jnp.arange(n_exp, dtype=jnp.int32)[None, None, :]).astype(jnp.int32)
    per_tok = onehot.sum(axis=1)
    counts = per_tok.sum(axis=0)
    pos = jnp.cumsum(per_tok, axis=0) - per_tok
    padded = ((counts + tile - 1) // tile) * tile
    ends = jnp.cumsum(padded)
    offs = ends - padded
    dest = jnp.take_along_axis(offs[None, :] + pos, eidx, axis=1)
    tok = jnp.broadcast_to(jnp.arange(M, dtype=jnp.int32)[:, None], (M, TOP_K))
    row_token = jnp.zeros((R,), jnp.int32).at[dest.reshape(-1)].set(tok.reshape(-1))
    row_gate = jnp.zeros((R,), F32).at[dest.reshape(-1)].set(gates.reshape(-1))
    tile_start = jnp.arange(n_tiles, dtype=jnp.int32) * tile
    tile_exp = jnp.minimum(jnp.searchsorted(ends, tile_start, side="right"), n_exp - 1).astype(jnp.int32)
    n_valid = (ends[-1] // tile).astype(jnp.int32).reshape(1)
    last = tile_exp[jnp.maximum(n_valid[0] - 1, 0)]
    tile_exp = jnp.where(tile_start < ends[-1], tile_exp, last)
    return row_token, row_gate.reshape(R, 1), dest.astype(jnp.int32), tile_exp, n_valid


def _moe_norm(x, xb_unused, router_w, w1, w3, w2, g, b, layer, moe_layer):
    n_exp = router_w.shape[2]
    gates, eidx = _router(x, router_w, moe_layer)
    row_token, row_gate, dest, tile_exp, n_valid = _moe_plan(eidx, gates, n_exp, MOE_ROW_TILE)
    xs = _gather_rows(x, row_token, MOE_ROW_TILE // 2)
    w1f = w1.reshape((-1,) + w1.shape[2:])
    w3f = w3.reshape((-1,) + w3.shape[2:])
    w2f = w2.reshape((-1,) + w2.shape[2:])
    gb = moe_layer * n_exp
    (hs,) = _matmul(xs, [w1f, w3f], gb, name="moe_up", tn=512, tm=MOE_ROW_TILE, epi=_epi_swiglu, out_dtypes=[BF16],
                    gid=tile_exp, n_valid=n_valid)
    gate_spec = (row_gate, (MOE_ROW_TILE, 1), lambda j, i, gid, nv: (i, 0))
    (ys,) = _matmul(hs, [w2f], gb, name="moe_down", tn=512, tm=MOE_ROW_TILE, epi=_epi_rowscale, out_dtypes=[F32],
                    extras=[gate_spec], gid=tile_exp, n_valid=n_valid)
    return _combine_norm(x, ys, dest, g, b, layer)


def _ple(x, xb, pb, ple_proj, ple_gate, layer, tm):
    tn = 256
    extras = [(x, (tm, tn), lambda j, i, g, nv: (i, j)),
              (pb, (tm, pb.shape[1]), lambda j, i, g, nv: (i, 0)),
              (ple_proj, (None, ple_proj.shape[1], tn), lambda j, i, g, nv: (layer, 0, j))]
    return _matmul(xb, [ple_gate], layer, name="ple", tn=tn, tm=tm, epi=_epi_ple, out_dtypes=[F32, BF16],
                   extras=extras)


def kernel(x_prompt, x_sample, p_prompt, p_sample, state_ret, state_conv, cache_k, cache_v, page_table,
           ln_mix_g, ln_mix_b, ln_ffn_g, ln_ffn_b, ple_proj, ple_gate, ev_w_in, ev_w_out, ret_gn_g, ret_gn_b,
           conv_w, conv_b, conv_ln_g, conv_ln_b, ffn_w1, ffn_w3, ffn_w2, sb_w_qkv, sb_w_out, sb_bias,
           moe_router, moe_w1, moe_w3, moe_w2):
    B, T, D = x_prompt.shape
    Bs, Ts, _ = x_sample.shape
    Mp, Ms = B * T, Bs * Ts
    M = Mp + Ms
    past = page_table.shape[1] * cache_k.shape[2]
    conv_width = conv_w.shape[2]
    d_ff = ffn_w1.shape[2]
    assert T % RET_CHUNK == 0 and Ts <= RET_CHUNK and Mp % RET_CHUNK == 0 and Mp % Ts == 0
    assert conv_width % CONV_LANES == 0 and (4 * RET_W) % CONV_LANES == 0 and d_ff % 2 == 0

    tm = _div_tile(M, 1376, BF16_SUBLANES)
    x = jnp.concatenate([x_prompt.reshape(Mp, D), x_sample.reshape(Ms, D)], axis=0)
    xb = x.astype(BF16)
    pb = jnp.concatenate([p_prompt.reshape(DEPTH, Mp, -1), p_sample.reshape(DEPTH, Ms, -1)], axis=1).astype(BF16)

    nc = T // RET_CHUNK
    cos_p, sin_p = _rope_tables(jnp.arange(T, dtype=jnp.int32), T)
    cos_s, sin_s = _rope_tables(past + jnp.arange(Ts, dtype=jnp.int32), RET_CHUNK)
    tab_p = _retention_tables(RET_CHUNK, RET_CHUNK)
    tab_s = _retention_tables(Ts, RET_CHUNK)
    zero_state = jnp.zeros((B, H_RET, DK_RET, DV_RET), F32)
    zero_buf = jnp.zeros((B, CONV_K - 1, conv_width), F32)
    conv_tt = min(128, T)

    ret_p, ret_s, conv_p, conv_s, kp, vp, ks, vs = [], [], [], [], [], [], [], []
    for i in range(DEPTH):
        if i % 2 == 0:
            e = i // 2
            (h,) = _matmul(xb, [ev_w_in], e, name="ev_in", tn=512, tm=tm, epi=_epi_store, out_dtypes=[F32])
            r_p, st_p = _retention(h, 0, B, nc, cos_p, sin_p, *tab_p, ret_gn_g, ret_gn_b, zero_state, e)
            hs_pad = jnp.pad(h[Mp:, :4 * RET_W].reshape(Bs, Ts, 4 * RET_W),
                             ((0, 0), (0, RET_CHUNK - Ts), (0, 0))).reshape(Bs * RET_CHUNK, 4 * RET_W)
            r_s, st_s = _retention(hs_pad, 0, Bs, 1, cos_s, sin_s, *tab_s, ret_gn_g, ret_gn_b, state_ret[e], e)
            r_s = r_s.reshape(Bs, RET_CHUNK, RET_W)[:, :Ts].reshape(Ms, RET_W)
            ca = (4 * RET_W) // CONV_LANES
            cb = ca + conv_width // CONV_LANES
            c_p, buf_p = _conv(h, 0, B, T // conv_tt, conv_tt, ca, cb, zero_buf, conv_w, conv_b, e)
            c_s, buf_s = _conv(h, Mp // Ts, Bs, 1, Ts, ca, cb, state_conv[e], conv_w, conv_b, e)
            cact = _norm_silu(jnp.concatenate([c_p, c_s], axis=0), conv_ln_g, conv_ln_b, e)
            mix = jnp.concatenate([jnp.concatenate([r_p, r_s], axis=0), cact], axis=1)
            (mo,) = _matmul(mix, [ev_w_out], e, name="ev_out", tn=512, tm=tm, epi=_epi_store, out_dtypes=[F32])
            x, xb = _post_norm(x, mo, ln_mix_g, ln_mix_b, i)
            (hf,) = _matmul(xb, [ffn_w1, ffn_w3], e, name="ffn_up", tn=256, tm=tm, epi=_epi_swiglu,
                            out_dtypes=[BF16])
            kh = d_ff // 2
            (f1,) = _matmul(hf, [ffn_w2], e, name="ffn_down_a", tn=256, tm=tm, epi=_epi_store, out_dtypes=[F32],
                            kpanel=(0, kh))
            add = [(f1, (tm, 256), lambda j, i_, g, nv: (i_, j))]
            (f2,) = _matmul(hf, [ffn_w2], e, name="ffn_down_b", tn=256, tm=tm, epi=_epi_add, out_dtypes=[F32], extras=add,
                            kpanel=(1, kh))
            x, xb = _post_norm(x, f2, ln_ffn_g, ln_ffn_b, i)
            ret_p.append(st_p)
            ret_s.append(st_s)
            conv_p.append(buf_p)
            conv_s.append(buf_s)
        else:
            o = i // 2
            (h,) = _matmul(xb, [sb_w_qkv], o, name="sb_qkv", tn=512, tm=tm, epi=_epi_store, out_dtypes=[F32])
            a_p = _sb_prompt(h, sb_bias[o], B, T)
            a_s = _sb_sample(h, Mp // Ts, cache_k, cache_v, page_table, sb_bias[o], o, Bs, Ts)
            att = jnp.concatenate([a_p, a_s.astype(BF16)], axis=0)
            (mo,) = _matmul(att, [sb_w_out], o, name="sb_out", tn=512, tm=tm, epi=_epi_store, out_dtypes=[F32])
            x, xb = _post_norm(x, mo, ln_mix_g, ln_mix_b, i)
            x, xb = _moe_norm(x, xb, moe_router, moe_w1, moe_w3, moe_w2, ln_ffn_g, ln_ffn_b, i, o)
            kp.append(h[:Mp, SB_W:2 * SB_W].reshape(B, T, H_SB, D_SB))
            vp.append(h[:Mp, 2 * SB_W:].reshape(B, T, H_SB, D_SB))
            ks.append(h[Mp:, SB_W:2 * SB_W].reshape(Bs, Ts, H_SB, D_SB))
            vs.append(h[Mp:, 2 * SB_W:].reshape(Bs, Ts, H_SB, D_SB))
        x, xb = _ple(x, xb, pb[i], ple_proj, ple_gate, i, tm)

    return (x[:Mp].reshape(B, T, D), x[Mp:].reshape(Bs, Ts, D), jnp.stack(ret_p), jnp.stack(ret_s),
            jnp.stack(conv_p), jnp.stack(conv_s), jnp.stack(kp), jnp.stack(vp), jnp.stack(ks), jnp.stack(vs))
```

```python
import functools
import math

import jax
import jax.numpy as jnp
from jax import lax
from jax.experimental import pallas as pl
from jax.experimental.pallas import tpu as pltpu

F32 = jnp.float32
BF16 = jnp.bfloat16

DEPTH = 4
H_RET = 8
DK_RET = 256
DV_RET = 256
RET_W = H_RET * DV_RET
RET_CHUNK = 128
ROPE_BASE = 10000.0
CONV_K = 31
H_SB = 32
D_SB = 128
SB_W = H_SB * D_SB
TOP_K = 2
ALPHA = (2.0 * DEPTH) ** 0.25
LN_EPS = 1e-5

V7X_VMEM_LIMIT_BYTES = 56 * 1024 * 1024
LANES = 128
BF16_SUBLANES = 16

MOE_ROW_TILE = 512
MOE_DOWN_SPLIT = 2
MOE_GATHER_TILE = 256
MOE_COMBINE_TILE = 192
SB_Q_TILE = 256
SB_HEADS_PER_STEP = 2
SB_PAGES_PER_STEP = 4
SB_HEAD_GROUP = 8
SB_Q_PAD = 16


def _cparams(*sem):
    return pltpu.CompilerParams(dimension_semantics=sem, vmem_limit_bytes=V7X_VMEM_LIMIT_BYTES)


def _div_tile(n, cap, mult):
    best = None
    for t in range(mult, min(n, cap) + 1, mult):
        if n % t == 0:
            best = t
    assert best is not None, (n, cap, mult)
    return best


def _sigmoid(x):
    return 1.0 / (1.0 + jnp.exp(-x))


def _silu(x):
    return x * _sigmoid(x)


def _layer_norm(y, g, b):
    mu = jnp.mean(y, axis=-1, keepdims=True)
    yc = y - mu
    var = jnp.mean(yc * yc, axis=-1, keepdims=True)
    return yc * lax.rsqrt(var + LN_EPS) * g + b


def _cast_panel(w_ref, wb_ref, kc):
    def body(c, carry):
        r = pl.multiple_of(c * kc, kc)
        wb_ref[pl.ds(r, kc), :] = w_ref[pl.ds(r, kc), :].astype(BF16)
        return carry
    lax.fori_loop(0, w_ref.shape[0] // kc, body, 0)


def _mm_kernel(gid_ref, nv_ref, x_ref, *rest, n_w, n_extra, epi, kc, cast):
    w_refs = rest[:n_w]
    extras = rest[n_w:n_w + n_extra]
    i = pl.program_id(1)
    if cast:
        outs = rest[n_w + n_extra:len(rest) - n_w]
        wbs = rest[len(rest) - n_w:]
        changed = jnp.logical_or(i == 0, gid_ref[i] != gid_ref[jnp.maximum(i - 1, 0)])

        @pl.when(changed)
        def _():
            for w_ref, wb in zip(w_refs, wbs):
                _cast_panel(w_ref, wb, kc)
    else:
        outs = rest[n_w + n_extra:]
        wbs = w_refs

    @pl.when(i < nv_ref[0])
    def _():
        x = x_ref[...]
        accs = [jnp.dot(x, wb[...], preferred_element_type=F32) for wb in wbs]
        epi(accs, extras, outs)

    @pl.when(i >= nv_ref[0])
    def _():
        for o in outs:
            o[...] = jnp.zeros(o.shape, o.dtype)


def _epi_store(accs, extras, outs):
    outs[0][...] = accs[0].astype(outs[0].dtype)


def _epi_swiglu(accs, extras, outs):
    outs[0][...] = (_silu(accs[0]) * accs[1]).astype(outs[0].dtype)


def _epi_rowscale(accs, extras, outs):
    outs[0][...] = accs[0] * extras[0][...]


def _epi_ple(accs, extras, outs):
    xres, p_ref, wp_ref = extras
    pp = jnp.dot(p_ref[...], wp_ref[...].astype(BF16), preferred_element_type=F32)
    y = xres[...] + pp * _sigmoid(accs[0])
    outs[0][...] = y
    outs[1][...] = y.astype(BF16)


def _matmul(x, ws, gbase, *, name, tn, tm, epi, out_dtypes, extras=(), gid=None, n_valid=None,
            single_buffer_w=False):
    M, K = x.shape
    N = ws[0].shape[2]
    assert all(w.shape[1] == K for w in ws)
    assert M % tm == 0 and N % tn == 0
    n_i = M // tm
    if gid is None:
        gid = jnp.zeros((n_i,), jnp.int32)
        n_valid = jnp.full((1,), n_i, jnp.int32)
    cast = ws[0].dtype != BF16
    kc = _div_tile(K, 1024, BF16_SUBLANES)
    w_mode = dict(pipeline_mode=pl.Buffered(1)) if single_buffer_w else {}
    in_specs = [pl.BlockSpec((tm, K), lambda j, i, g, nv: (i, 0))]
    for _ in ws:
        in_specs.append(pl.BlockSpec((None, K, tn), lambda j, i, g, nv: (gbase + g[i], 0, j), **w_mode))
    for _, bs, im in extras:
        in_specs.append(pl.BlockSpec(bs, im))
    out_shape = [jax.ShapeDtypeStruct((M, N), dt) for dt in out_dtypes]
    out_specs = [pl.BlockSpec((tm, tn), lambda j, i, g, nv: (i, j)) for _ in out_dtypes]
    kern = functools.partial(_mm_kernel, n_w=len(ws), n_extra=len(extras), epi=epi, kc=kc, cast=cast)
    res = pl.pallas_call(
        kern,
        grid_spec=pltpu.PrefetchScalarGridSpec(
            num_scalar_prefetch=2, grid=(N // tn, n_i), in_specs=in_specs, out_specs=out_specs,
            scratch_shapes=[pltpu.VMEM((K, tn), BF16) for _ in ws] if cast else []),
        out_shape=out_shape, name=name,
        compiler_params=_cparams("arbitrary", "arbitrary"),
    )(gid, n_valid, x, *ws, *[e[0] for e in extras])
    return res


def _post_norm_kernel(x_ref, s_ref, g_ref, b_ref, o_ref, ob_ref):
    y = _layer_norm(ALPHA * x_ref[...] + s_ref[...], g_ref[...], b_ref[...])
    o_ref[...] = y
    ob_ref[...] = y.astype(BF16)


def _post_norm(x, sub, g, b, layer):
    M, D = x.shape
    tm = _div_tile(M, 256, BF16_SUBLANES)
    g3 = g.reshape(g.shape[0], 1, D)
    b3 = b.reshape(b.shape[0], 1, D)
    row = pl.BlockSpec((tm, D), lambda i: (i, 0))
    vec = pl.BlockSpec((None, 1, D), lambda i: (layer, 0, 0))
    return pl.pallas_call(
        _post_norm_kernel, grid=(M // tm,), in_specs=[row, row, vec, vec], out_specs=[row, row],
        out_shape=[jax.ShapeDtypeStruct((M, D), F32), jax.ShapeDtypeStruct((M, D), BF16)],
        name="post_norm", compiler_params=_cparams("arbitrary"),
    )(x, sub, g3, b3)


def _norm_silu_kernel(c_ref, g_ref, b_ref, o_ref):
    o_ref[...] = _silu(_layer_norm(c_ref[...], g_ref[...], b_ref[...])).astype(BF16)


def _norm_silu(c, g, b, layer):
    M, D = c.shape
    tm = _div_tile(M, 512, BF16_SUBLANES)
    g3 = g.reshape(g.shape[0], 1, D)
    b3 = b.reshape(b.shape[0], 1, D)
    row = pl.BlockSpec((tm, D), lambda i: (i, 0))
    vec = pl.BlockSpec((None, 1, D), lambda i: (layer, 0, 0))
    return pl.pallas_call(
        _norm_silu_kernel, grid=(M // tm,), in_specs=[row, vec, vec], out_specs=row,
        out_shape=jax.ShapeDtypeStruct((M, D), BF16), name="conv_norm_silu",
        compiler_params=_cparams("arbitrary"),
    )(c, g3, b3)


def _retention_kernel(q_ref, k_ref, v_ref, g_ref, cos_ref, sin_ref, dec_ref, qs_ref, ks_ref, rs_ref,
                      gg_ref, gb_ref, r0_ref, o_ref, st_ref, r_scr, *, n_chunks):
    c = pl.program_id(2)

    @pl.when(c == 0)
    def _():
        r_scr[...] = r0_ref[...]

    cos = cos_ref[...]
    sin = sin_ref[...]
    half = DK_RET // 2

    def rot(x):
        x1 = x[:, :half]
        x2 = x[:, half:]
        return jnp.concatenate([x1 * cos - x2 * sin, x1 * sin + x2 * cos], axis=-1)

    qr = rot(q_ref[...])
    kr = rot(k_ref[...]) * (DK_RET ** -0.5)
    qb = qr.astype(BF16)
    kb = kr.astype(BF16)
    vb = v_ref[...].astype(BF16)
    s = lax.dot_general(qb, kb, (((1,), (1,)), ((), ())), preferred_element_type=F32) * dec_ref[...]
    inner = jnp.dot(s.astype(BF16), vb, preferred_element_type=F32)
    r = r_scr[...]
    cross = jnp.dot(qb, r.astype(BF16), preferred_element_type=F32) * qs_ref[...]
    ksc = (kr * ks_ref[...]).astype(BF16)
    r_new = r * rs_ref[...] + lax.dot_general(ksc, vb, (((0,), (0,)), ((), ())), preferred_element_type=F32)
    r_scr[...] = r_new
    y = _layer_norm(inner + cross, gg_ref[...], gb_ref[...])
    o_ref[...] = (_silu(g_ref[...]) * y).astype(o_ref.dtype)

    @pl.when(c == n_chunks - 1)
    def _():
        st_ref[...] = r_new


def _retention(h, row_blk0, n_batch, n_chunks, cos, sin, decay, qsc, ksc, rsc, gn_g, gn_b, r0, layer):
    C = RET_CHUNK
    hq = DK_RET
    ncb = RET_W // hq

    def rows(col0):
        return pl.BlockSpec((C, hq), lambda b, hh, c: (row_blk0 + b * n_chunks + c, col0 + hh))

    tab = pl.BlockSpec((C, DK_RET // 2), lambda b, hh, c: (c, 0))
    per_head = lambda shp: pl.BlockSpec((None,) + shp, lambda b, hh, c: (hh, 0, 0))
    gvec = pl.BlockSpec((None, 1, hq), lambda b, hh, c: (layer, 0, hh))
    state = pl.BlockSpec((None, None, DK_RET, DV_RET), lambda b, hh, c: (b, hh, 0, 0))
    out_rows = n_batch * n_chunks * C
    return pl.pallas_call(
        functools.partial(_retention_kernel, n_chunks=n_chunks),
        grid=(n_batch, H_RET, n_chunks),
        in_specs=[rows(0), rows(ncb), rows(2 * ncb), rows(3 * ncb), tab, tab,
                  per_head((C, C)), per_head((C, 1)), per_head((C, 1)), per_head((1, DV_RET)),
                  gvec, gvec, state],
        out_specs=[pl.BlockSpec((C, hq), lambda b, hh, c: (b * n_chunks + c, hh)), state],
        out_shape=[jax.ShapeDtypeStruct((out_rows, RET_W), BF16),
                   jax.ShapeDtypeStruct((n_batch, H_RET, DK_RET, DV_RET), F32)],
        scratch_shapes=[pltpu.VMEM((DK_RET, DV_RET), F32)], name="retention",
        compiler_params=_cparams("arbitrary", "arbitrary", "arbitrary"),
    )(h, h, h, h, cos, sin, decay, qsc, ksc, rsc,
      gn_g.reshape(gn_g.shape[0], 1, RET_W), gn_b.reshape(gn_b.shape[0], 1, RET_W), r0)


def _retention_tables(chunk, pad_to):
    log_gamma = jnp.log1p(-jnp.exp2(-5.0 - jnp.arange(H_RET, dtype=F32)))
    idx = jnp.arange(chunk, dtype=F32)
    lg = log_gamma[:, None]
    diff = idx[:, None] - idx[None, :]
    decay = jnp.where(diff >= 0, jnp.exp(lg[:, :, None] * jnp.maximum(diff, 0.0)), 0.0)
    qsc = jnp.exp(lg * (idx + 1.0))[:, :, None]
    ksc = jnp.exp(lg * (chunk - 1.0 - idx))[:, :, None]
    rsc = jnp.broadcast_to(jnp.exp(log_gamma * chunk)[:, None, None], (H_RET, 1, DV_RET))
    p = pad_to - chunk
    decay = jnp.pad(decay, ((0, 0), (0, p), (0, p)))
    qsc = jnp.pad(qsc, ((0, 0), (0, p), (0, 0)))
    ksc = jnp.pad(ksc, ((0, 0), (0, p), (0, 0)))
    return decay, qsc, ksc, rsc


def _rope_tables(pos, pad_to):
    half = DK_RET // 2
    inv = jnp.exp(-math.log(ROPE_BASE) * jnp.arange(half, dtype=F32) / half)
    ang = pos.astype(F32)[:, None] * inv[None, :]
    p = pad_to - pos.shape[0]
    return jnp.pad(jnp.cos(ang), ((0, p), (0, 0))), jnp.pad(jnp.sin(ang), ((0, p), (0, 0)))


CONV_HIST = 32
CONV_LANES = 512
CONV_ROWS = 64


def _conv_kernel(a_ref, b_ref, st_ref, w_ref, bias_ref, o_ref, buf_ref, full, *, tt, n_t):
    t = pl.program_id(2)
    kh = CONV_K - 1
    off = CONV_HIST - kh

    @pl.when(t == 0)
    def _():
        full[0:off, :] = jnp.zeros((off, CONV_LANES), F32)
        full[off:CONV_HIST, :] = st_ref[...]

    @pl.when(t > 0)
    def _():
        full[0:CONV_HIST, :] = full[tt:tt + CONV_HIST, :]

    full[CONV_HIST:CONV_HIST + tt, :] = a_ref[...] * _sigmoid(b_ref[...])
    rc = min(CONV_ROWS, tt)
    for r0 in range(0, tt, rc):
        acc = jnp.zeros((rc, CONV_LANES), F32) + bias_ref[...]
        for j in range(CONV_K):
            acc = acc + w_ref[j:j + 1, :] * full[off + r0 + j:off + r0 + j + rc, :]
        o_ref[r0:r0 + rc, :] = acc

    @pl.when(t == n_t - 1)
    def _():
        buf_ref[...] = full[CONV_HIST + tt - kh:CONV_HIST + tt, :]


def _conv(h, row_blk0, n_batch, n_t, tt, col_a, col_b, state, w, bias, layer):
    W = w.shape[2]
    n_l = W // CONV_LANES
    kh = CONV_K - 1
    rows = lambda col0: pl.BlockSpec((tt, CONV_LANES), lambda b, l, t: (row_blk0 + b * n_t + t, col0 + l))
    return pl.pallas_call(
        functools.partial(_conv_kernel, tt=tt, n_t=n_t),
        grid=(n_batch, n_l, n_t),
        in_specs=[rows(col_a), rows(col_b),
                  pl.BlockSpec((None, kh, CONV_LANES), lambda b, l, t: (b, 0, l)),
                  pl.BlockSpec((None, CONV_K, CONV_LANES), lambda b, l, t: (layer, 0, l)),
                  pl.BlockSpec((None, 1, CONV_LANES), lambda b, l, t: (layer, 0, l))],
        out_specs=[pl.BlockSpec((tt, CONV_LANES), lambda b, l, t: (b * n_t + t, l)),
                   pl.BlockSpec((None, kh, CONV_LANES), lambda b, l, t: (b, 0, l))],
        out_shape=[jax.ShapeDtypeStruct((n_batch * n_t * tt, W), F32),
                   jax.ShapeDtypeStruct((n_batch, kh, W), F32)],
        scratch_shapes=[pltpu.VMEM((CONV_HIST + tt, CONV_LANES), F32)], name="glu_conv",
        compiler_params=_cparams("arbitrary", "arbitrary", "arbitrary"),
    )(h, h, state, w, bias.reshape(bias.shape[0], 1, W))


def _sb_block(z, mask, carry, tri):
    w = jnp.maximum(z, 0.0) + jnp.log(1.0 + jnp.exp(-jnp.abs(z)))
    if mask is not None:
        w = jnp.where(mask, w, 0.0)
    w_hi = w.astype(BF16)
    w_lo = (w - w_hi.astype(F32)).astype(BF16)
    tail = jnp.dot(w_hi, tri, preferred_element_type=F32) + jnp.dot(w_lo, tri, preferred_element_type=F32)
    a = jnp.exp(z - tail - carry)
    if mask is not None:
        a = jnp.where(mask, a, 0.0)
    return a, carry + tail[:, :1]


def _tri(n):
    r = jnp.arange(n)
    return (r[:, None] >= r[None, :]).astype(BF16)


def _sb_prompt_kernel(bias_ref, q_ref, k_ref, v_ref, tri_ref, o_ref, *, tq):
    nh = SB_HEADS_PER_STEP
    hp = pl.program_id(1)
    qi = pl.program_id(2)
    scale = D_SB ** -0.5
    tri = tri_ref[...]
    cols = [slice(g * D_SB, (g + 1) * D_SB) for g in range(nh)]
    qbs = [q_ref[:, cols[g]].astype(BF16) for g in range(nh)]
    biases = [bias_ref[hp * nh + g] for g in range(nh)]

    def key_block(kb):
        return pl.multiple_of(jnp.maximum(kb, 0) * tq, tq)

    def logits(r0):
        return tuple(lax.dot_general(qbs[g], k_ref[pl.ds(r0, tq), cols[g]].astype(BF16), (((1,), (1,)), ((), ())),
                                     preferred_element_type=F32) for g in range(nh))

    def visit(r0, zs, mask, state):
        new = []
        for g in range(nh):
            carry, o = state[g]
            vb = v_ref[pl.ds(r0, tq), cols[g]].astype(BF16)
            a, carry = _sb_block(zs[g] * scale + biases[g], mask, carry, tri)
            new.append((carry, o + jnp.dot(a.astype(BF16), vb, preferred_element_type=F32)))
        return tuple(new)

    row = lax.broadcasted_iota(jnp.int32, (tq, tq), 0)
    col = lax.broadcasted_iota(jnp.int32, (tq, tq), 1)
    state = tuple((jnp.zeros((tq, 1), F32), jnp.zeros((tq, D_SB), F32)) for _ in range(nh))
    zs_next = logits(key_block(qi - 1))
    state = visit(key_block(qi), logits(key_block(qi)), col < row, state)

    def body(kk, st):
        zs, state = st
        return logits(key_block(qi - kk - 1)), visit(key_block(qi - kk), zs, None, state)

    _, state = lax.fori_loop(1, qi + 1, body, (zs_next, state))
    for g in range(nh):
        o_ref[:, cols[g]] = state[g][1].astype(BF16)


def _sb_prompt(h, bias, n_batch, seq):
    tq = min(SB_Q_TILE, seq)
    nq = seq // tq
    wide = SB_HEADS_PER_STEP * D_SB
    n_hp = H_SB // SB_HEADS_PER_STEP
    return pl.pallas_call(
        functools.partial(_sb_prompt_kernel, tq=tq),
        grid_spec=pltpu.PrefetchScalarGridSpec(
            num_scalar_prefetch=1, grid=(n_batch, n_hp, nq),
            in_specs=[pl.BlockSpec((tq, wide), lambda b, hp, qi, bias: (b * nq + qi, hp)),
                      pl.BlockSpec((seq, wide), lambda b, hp, qi, bias: (b, n_hp + hp)),
                      pl.BlockSpec((seq, wide), lambda b, hp, qi, bias: (b, 2 * n_hp + hp)),
                      pl.BlockSpec((tq, tq), lambda b, hp, qi, bias: (0, 0))],
            out_specs=pl.BlockSpec((tq, wide), lambda b, hp, qi, bias: (b * nq + qi, hp))),
        out_shape=jax.ShapeDtypeStruct((n_batch * seq, SB_W), BF16), name="sb_prompt",
        compiler_params=_cparams("arbitrary", "arbitrary", "arbitrary"),
    )(bias, h, h, h, _tri(tq))


def _sb_sample_kernel(pt_ref, bias_ref, q_ref, kn_ref, vn_ref, *rest, n_steps, t_dec, page):
    P = SB_PAGES_PER_STEP
    HG = SB_HEAD_GROUP
    k_pages = rest[:P]
    v_pages = rest[P:2 * P]
    tri_ref, o_ref, qs, carry, acc = rest[2 * P:]
    hg = pl.program_id(1)
    s = pl.program_id(2)
    scale = D_SB ** -0.5
    QP = SB_Q_PAD
    rows = HG * QP
    lanes = lambda j: slice(j * D_SB, (j + 1) * D_SB)

    def bias_col():
        hrow = lax.broadcasted_iota(jnp.int32, (rows, 1), 0) // QP
        out = jnp.zeros((rows, 1), F32)
        for j in range(HG):
            out = jnp.where(hrow == j, bias_ref[hg * HG + j], out)
        return out

    def process(kcat, vcat, mask, tri):
        z = jnp.concatenate(
            [lax.dot_general(qs[j], kcat[j], (((1,), (1,)), ((), ())), preferred_element_type=F32)
             for j in range(HG)], axis=0)
        a, new_carry = _sb_block(z * scale + bias_col(), mask, carry[...], tri)
        carry[...] = new_carry
        ab = a.astype(BF16)
        for j in range(HG):
            acc[j * QP:(j + 1) * QP, :] += jnp.dot(ab[j * QP:(j + 1) * QP, :], vcat[j], preferred_element_type=F32)

    @pl.when(s == 0)
    def _():
        carry[...] = jnp.zeros_like(carry)
        acc[...] = jnp.zeros_like(acc)
        pad = jnp.zeros((QP - t_dec, D_SB), F32)
        for j in range(HG):
            qs[j] = jnp.concatenate([q_ref[:, lanes(j)], pad], axis=0).astype(BF16)
        kpad = jnp.zeros((page - t_dec, D_SB), F32)
        kcat = [jnp.concatenate([kn_ref[:, lanes(j)], kpad], axis=0).astype(BF16) for j in range(HG)]
        vcat = [jnp.concatenate([vn_ref[:, lanes(j)], kpad], axis=0).astype(BF16) for j in range(HG)]
        trow = lax.broadcasted_iota(jnp.int32, (rows, page), 0) % QP
        col = lax.broadcasted_iota(jnp.int32, (rows, page), 1)
        process(kcat, vcat, col < trow, tri_ref[0:page, 0:page])

    @pl.when(s > 0)
    def _():
        kts = [pltpu.einshape("khd->hkd", kp[...]) for kp in k_pages]
        vts = [pltpu.einshape("khd->hkd", vp[...]) for vp in v_pages]
        kcat = [jnp.concatenate([kt[j] for kt in kts], axis=0).astype(BF16) for j in range(HG)]
        vcat = [jnp.concatenate([vt[j] for vt in vts], axis=0).astype(BF16) for j in range(HG)]
        process(kcat, vcat, None, tri_ref[...])

    @pl.when(s == n_steps - 1)
    def _():
        for j in range(HG):
            o_ref[:, lanes(j)] = acc[j * QP:j * QP + t_dec, :]


def _sb_sample(h, row_blk0, cache_k, cache_v, page_table, bias, layer, n_batch, t_dec):
    P = SB_PAGES_PER_STEP
    HG = SB_HEAD_GROUP
    n_pages = page_table.shape[1]
    page = cache_k.shape[2]
    assert n_pages % P == 0 and t_dec <= SB_Q_PAD and t_dec <= page and H_SB % HG == 0
    n_groups = n_pages // P
    n_steps = n_groups + 1
    n_hg = H_SB // HG
    wide = HG * D_SB

    def page_spec(i):
        def im(b, hg, s, pt, bias):
            g = n_groups - jnp.maximum(s, 1)
            return (layer, pt[b * n_pages + g * P + i], 0, hg, 0)
        return pl.BlockSpec((None, None, page, HG, D_SB), im)

    new = lambda part: pl.BlockSpec((t_dec, wide), lambda b, hg, s, pt, bias: (row_blk0 + b, part * n_hg + hg))
    return pl.pallas_call(
        functools.partial(_sb_sample_kernel, n_steps=n_steps, t_dec=t_dec, page=page),
        grid_spec=pltpu.PrefetchScalarGridSpec(
            num_scalar_prefetch=2, grid=(n_batch, n_hg, n_steps),
            in_specs=[new(0), new(1), new(2)] + [page_spec(i) for i in range(P)] * 2
            + [pl.BlockSpec((P * page, P * page), lambda b, hg, s, pt, bias: (0, 0))],
            out_specs=pl.BlockSpec((t_dec, wide), lambda b, hg, s, pt, bias: (b, hg)),
            scratch_shapes=[pltpu.VMEM((HG, SB_Q_PAD, D_SB), BF16),
                            pltpu.VMEM((HG * SB_Q_PAD, 1), F32),
                            pltpu.VMEM((HG * SB_Q_PAD, D_SB), F32)]),
        out_shape=jax.ShapeDtypeStruct((n_batch * t_dec, SB_W), F32), name="sb_sample",
        compiler_params=_cparams("arbitrary", "arbitrary", "arbitrary"),
    )(page_table.reshape(-1), bias, h, h, h, *([cache_k] * P), *([cache_v] * P), _tri(P * page))


def _split_bf16(x):
    hi = x.astype(BF16)
    return hi, (x - hi.astype(F32)).astype(BF16)


def _router_kernel(x_ref, w_ref, g_ref, e_ref, *, n_exp):
    xh, xl = _split_bf16(x_ref[...])
    wh, wl = _split_bf16(w_ref[...])
    dot = functools.partial(jnp.dot, preferred_element_type=F32)
    logits = dot(xh, wh) + dot(xl, wh) + dot(xh, wl)
    lane = lax.broadcasted_iota(jnp.int32, logits.shape, 1)
    lanef = lane.astype(F32)
    neg = jnp.float32(-jnp.inf)
    logits = jnp.where(lane < n_exp, logits, neg)
    m1 = jnp.max(logits, axis=-1, keepdims=True)
    i1 = jnp.min(jnp.where(logits == m1, lanef, float(LANES)), axis=-1, keepdims=True)
    rest = jnp.where(lanef == i1, neg, logits)
    m2 = jnp.max(rest, axis=-1, keepdims=True)
    i2 = jnp.min(jnp.where(rest == m2, lanef, float(LANES)), axis=-1, keepdims=True)
    e2 = jnp.exp(m2 - m1)
    den = 1.0 + e2
    g_ref[...] = jnp.where(lane == 0, 1.0 / den, jnp.where(lane == 1, e2 / den, 0.0))
    e_ref[...] = jnp.where(lane == 0, i1, jnp.where(lane == 1, i2, 0.0)).astype(jnp.int32)


def _router(x, router_w, layer):
    M, D = x.shape
    n_exp = router_w.shape[2]
    tm = _div_tile(M, 256, 8)
    wpad = jnp.pad(router_w, ((0, 0), (0, 0), (0, LANES - n_exp)))
    row = lambda w: pl.BlockSpec((tm, w), lambda i: (i, 0))
    gates, eidx = pl.pallas_call(
        functools.partial(_router_kernel, n_exp=n_exp), grid=(M // tm,),
        in_specs=[row(D), pl.BlockSpec((None, D, LANES), lambda i: (layer, 0, 0))],
        out_specs=[row(LANES), row(LANES)],
        out_shape=[jax.ShapeDtypeStruct((M, LANES), F32), jax.ShapeDtypeStruct((M, LANES), jnp.int32)],
        name="moe_router", compiler_params=_cparams("arbitrary"),
    )(x, wpad)
    return gates[:, :TOP_K], eidx[:, :TOP_K]


def _row_copy(src_hbm, dst, src_row, dst_row, sem):
    return pltpu.make_async_copy(src_hbm.at[pl.ds(src_row, 1), :], dst.at[pl.ds(dst_row, 1), :], sem)


def _gather_kernel(tok_ref, x_hbm, o_ref, buf, sem, *, tg):
    i = pl.program_id(0)

    def issue(r, carry):
        _row_copy(x_hbm, buf, tok_ref[i * tg + r], r, sem).start()
        return carry

    def wait(r, carry):
        _row_copy(x_hbm, buf, 0, r, sem).wait()
        return carry

    lax.fori_loop(0, tg, issue, 0)
    lax.fori_loop(0, tg, wait, 0)
    o_ref[...] = buf[...].astype(BF16)


def _gather_rows(x, row_token, tg):
    R = row_token.shape[0]
    D = x.shape[1]
    return pl.pallas_call(
        functools.partial(_gather_kernel, tg=tg),
        grid_spec=pltpu.PrefetchScalarGridSpec(
            num_scalar_prefetch=1, grid=(R // tg,),
            in_specs=[pl.BlockSpec(memory_space=pl.ANY)],
            out_specs=pl.BlockSpec((tg, D), lambda i, tok: (i, 0)),
            scratch_shapes=[pltpu.VMEM((tg, D), F32), pltpu.SemaphoreType.DMA(())]),
        out_shape=jax.ShapeDtypeStruct((R, D), BF16), name="moe_gather",
        compiler_params=_cparams("arbitrary"),
    )(row_token, x)


def _combine_norm_kernel(dest_ref, x_ref, g_ref, b_ref, y_hbm, o_ref, ob_ref, buf, sem, *, tc):
    i = pl.program_id(0)

    def issue(r, carry):
        for k in range(TOP_K):
            _row_copy(y_hbm, buf.at[k], dest_ref[(i * tc + r) * TOP_K + k], r, sem).start()
        return carry

    def wait(r, carry):
        for k in range(TOP_K):
            _row_copy(y_hbm, buf.at[k], 0, r, sem).wait()
        return carry

    lax.fori_loop(0, tc, issue, 0)
    lax.fori_loop(0, tc, wait, 0)
    moe = buf[0] + buf[1]
    y = _layer_norm(ALPHA * x_ref[...] + moe, g_ref[...], b_ref[...])
    o_ref[...] = y
    ob_ref[...] = y.astype(BF16)


def _combine_norm(x, y_rows, dest, g, b, layer):
    M, D = x.shape
    tc = _div_tile(M, MOE_COMBINE_TILE, BF16_SUBLANES)
    g3 = g.reshape(g.shape[0], 1, D)
    b3 = b.reshape(b.shape[0], 1, D)
    row = pl.BlockSpec((tc, D), lambda i, d: (i, 0))
    vec = pl.BlockSpec((None, 1, D), lambda i, d: (layer, 0, 0))
    return pl.pallas_call(
        functools.partial(_combine_norm_kernel, tc=tc),
        grid_spec=pltpu.PrefetchScalarGridSpec(
            num_scalar_prefetch=1, grid=(M // tc,),
            in_specs=[row, vec, vec, pl.BlockSpec(memory_space=pl.ANY)],
            out_specs=[row, row],
            scratch_shapes=[pltpu.VMEM((TOP_K, tc, D), F32), pltpu.SemaphoreType.DMA(())]),
        out_shape=[jax.ShapeDtypeStruct((M, D), F32), jax.ShapeDtypeStruct((M, D), BF16)],
        name="moe_combine_norm", compiler_params=_cparams("arbitrary"),
    )(dest.reshape(-1), x, g3, b3, y_rows)


def _moe_plan(eidx, gates, n_exp, tile):
    M = eidx.shape[0]
    n_tiles = (M * TOP_K + n_exp * (tile - 1)) // tile
    R = n_tiles * tile
    onehot = (eidx[:, :, None] == jnp.arange(n_exp, dtype=jnp.int32)[None, None, :]).astype(jnp.int32)
    per_tok = onehot.sum(axis=1)
    counts = per_tok.sum(axis=0)
    pos = jnp.cumsum(per_tok, axis=0) - per_tok
    padded = ((counts + tile - 1) // tile) * tile
    ends = jnp.cumsum(padded)
    offs = ends - padded
    dest = jnp.take_along_axis(offs[None, :] + pos, eidx, axis=1)
    tok = jnp.broadcast_to(jnp.arange(M, dtype=jnp.int32)[:, None], (M, TOP_K))
    row_token = jnp.zeros((R,), jnp.int32).at[dest.reshape(-1)].set(tok.reshape(-1))
    row_gate = jnp.zeros((R,), F32).at[dest.reshape(-1)].set(gates.reshape(-1))
    tile_start = jnp.arange(n_tiles, dtype=jnp.int32) * tile
    tile_exp = jnp.minimum(jnp.searchsorted(ends, tile_start, side="right"), n_exp - 1).astype(jnp.int32)
    n_valid = (ends[-1] // tile).astype(jnp.int32).reshape(1)
    last = tile_exp[jnp.maximum(n_valid[0] - 1, 0)]
    tile_exp = jnp.where(tile_start < ends[-1], tile_exp, last)
    return row_token, row_gate.reshape(R, 1), dest.astype(jnp.int32), tile_exp, n_valid


def _moe_norm(x, xb_unused, router_w, w1, w3, w2, g, b, layer, moe_layer):
    n_exp = router_w.shape[2]
    gates, eidx = _router(x, router_w, moe_layer)
    row_token, row_gate, dest, tile_exp, n_valid = _moe_plan(eidx, gates, n_exp, MOE_ROW_TILE)
    xs = _gather_rows(x, row_token, MOE_GATHER_TILE)
    w1f = w1.reshape((-1,) + w1.shape[2:])
    w3f = w3.reshape((-1,) + w3.shape[2:])
    w2f = w2.reshape((-1,) + w2.shape[2:])
    gb = moe_layer * n_exp
    (hs,) = _matmul(xs, [w1f, w3f], gb, name="moe_up", tn=512, tm=MOE_ROW_TILE, epi=_epi_swiglu, out_dtypes=[BF16],
                    gid=tile_exp, n_valid=n_valid)
    tm_dn = MOE_ROW_TILE // MOE_DOWN_SPLIT
    gate_spec = (row_gate, (tm_dn, 1), lambda j, i, gid, nv: (i, 0))
    (ys,) = _matmul(hs, [w2f], gb, name="moe_down", tn=512, tm=tm_dn, epi=_epi_rowscale, out_dtypes=[F32],
                    extras=[gate_spec], gid=jnp.repeat(tile_exp, MOE_DOWN_SPLIT), n_valid=n_valid * MOE_DOWN_SPLIT)
    return _combine_norm(x, ys, dest, g, b, layer)


def _ple(x, xb, pb, ple_proj, ple_gate, layer, tm):
    tn = 512
    extras = [(x, (tm, tn), lambda j, i, g, nv: (i, j)),
              (pb, (tm, pb.shape[1]), lambda j, i, g, nv: (i, 0)),
              (ple_proj, (None, ple_proj.shape[1], tn), lambda j, i, g, nv: (layer, 0, j))]
    return _matmul(xb, [ple_gate], layer, name="ple", tn=tn, tm=tm, epi=_epi_ple, out_dtypes=[F32, BF16],
                   extras=extras, single_buffer_w=True)


def kernel(x_prompt, x_sample, p_prompt, p_sample, state_ret, state_conv, cache_k, cache_v, page_table,
           ln_mix_g, ln_mix_b, ln_ffn_g, ln_ffn_b, ple_proj, ple_gate, ev_w_in, ev_w_out, ret_gn_g, ret_gn_b,
           conv_w, conv_b, conv_ln_g, conv_ln_b, ffn_w1, ffn_w3, ffn_w2, sb_w_qkv, sb_w_out, sb_bias,
           moe_router, moe_w1, moe_w3, moe_w2):
    B, T, D = x_prompt.shape
    Bs, Ts, _ = x_sample.shape
    Mp, Ms = B * T, Bs * Ts
    M = Mp + Ms
    past = page_table.shape[1] * cache_k.shape[2]
    conv_width = conv_w.shape[2]
    assert T % RET_CHUNK == 0 and Ts <= RET_CHUNK and Mp % RET_CHUNK == 0 and Mp % Ts == 0
    assert conv_width % CONV_LANES == 0 and (4 * RET_W) % CONV_LANES == 0

    tm = _div_tile(M, 1376, BF16_SUBLANES)
    tm_n = _div_tile(M, 688, BF16_SUBLANES)
    ffn_w2_bf16 = ffn_w2.astype(BF16)
    x = jnp.concatenate([x_prompt.reshape(Mp, D), x_sample.reshape(Ms, D)], axis=0)
    xb = x.astype(BF16)
    pb = jnp.concatenate([p_prompt.reshape(DEPTH, Mp, -1), p_sample.reshape(DEPTH, Ms, -1)], axis=1).astype(BF16)

    nc = T // RET_CHUNK
    cos_p, sin_p = _rope_tables(jnp.arange(T, dtype=jnp.int32), T)
    cos_s, sin_s = _rope_tables(past + jnp.arange(Ts, dtype=jnp.int32), RET_CHUNK)
    tab_p = _retention_tables(RET_CHUNK, RET_CHUNK)
    tab_s = _retention_tables(Ts, RET_CHUNK)
    zero_state = jnp.zeros((B, H_RET, DK_RET, DV_RET), F32)
    zero_buf = jnp.zeros((B, CONV_K - 1, conv_width), F32)
    conv_tt = min(128, T)

    ret_p, ret_s, conv_p, conv_s, kp, vp, ks, vs = [], [], [], [], [], [], [], []
    for i in range(DEPTH):
        if i % 2 == 0:
            e = i // 2
            (h,) = _matmul(xb, [ev_w_in], e, name="ev_in", tn=512, tm=tm, epi=_epi_store, out_dtypes=[F32])
            r_p, st_p = _retention(h, 0, B, nc, cos_p, sin_p, *tab_p, ret_gn_g, ret_gn_b, zero_state, e)
            hs_pad = jnp.pad(h[Mp:, :4 * RET_W].reshape(Bs, Ts, 4 * RET_W),
                             ((0, 0), (0, RET_CHUNK - Ts), (0, 0))).reshape(Bs * RET_CHUNK, 4 * RET_W)
            r_s, st_s = _retention(hs_pad, 0, Bs, 1, cos_s, sin_s, *tab_s, ret_gn_g, ret_gn_b, state_ret[e], e)
            r_s = r_s.reshape(Bs, RET_CHUNK, RET_W)[:, :Ts].reshape(Ms, RET_W)
            ca = (4 * RET_W) // CONV_LANES
            cb = ca + conv_width // CONV_LANES
            c_p, buf_p = _conv(h, 0, B, T // conv_tt, conv_tt, ca, cb, zero_buf, conv_w, conv_b, e)
            c_s, buf_s = _conv(h, Mp // Ts, Bs, 1, Ts, ca, cb, state_conv[e], conv_w, conv_b, e)
            cact = _norm_silu(jnp.concatenate([c_p, c_s], axis=0), conv_ln_g, conv_ln_b, e)
            mix = jnp.concatenate([jnp.concatenate([r_p, r_s], axis=0), cact], axis=1)
            (mo,) = _matmul(mix, [ev_w_out], e, name="ev_out", tn=512, tm=tm, epi=_epi_store, out_dtypes=[F32])
            x, xb = _post_norm(x, mo, ln_mix_g, ln_mix_b, i)
            (hf,) = _matmul(xb, [ffn_w1, ffn_w3], e, name="ffn_up", tn=256, tm=tm, epi=_epi_swiglu,
                            out_dtypes=[BF16])
            (f2,) = _matmul(hf, [ffn_w2_bf16], e, name="ffn_down", tn=512, tm=tm_n, epi=_epi_store, out_dtypes=[F32],
                            single_buffer_w=True)
            x, xb = _post_norm(x, f2, ln_ffn_g, ln_ffn_b, i)
            ret_p.append(st_p)
            ret_s.append(st_s)
            conv_p.append(buf_p)
            conv_s.append(buf_s)
        else:
            o = i // 2
            (h,) = _matmul(xb, [sb_w_qkv], o, name="sb_qkv", tn=512, tm=tm, epi=_epi_store, out_dtypes=[F32])
            a_p = _sb_prompt(h, sb_bias[o], B, T)
            a_s = _sb_sample(h, Mp // Ts, cache_k, cache_v, page_table, sb_bias[o], o, Bs, Ts)
            att = jnp.concatenate([a_p, a_s.astype(BF16)], axis=0)
            (mo,) = _matmul(att, [sb_w_out], o, name="sb_out", tn=512, tm=tm, epi=_epi_store, out_dtypes=[F32])
            x, xb = _post_norm(x, mo, ln_mix_g, ln_mix_b, i)
            x, xb = _moe_norm(x, xb, moe_router, moe_w1, moe_w3, moe_w2, ln_ffn_g, ln_ffn_b, i, o)
            kp.append(h[:Mp, SB_W:2 * SB_W].reshape(B, T, H_SB, D_SB))
            vp.append(h[:Mp, 2 * SB_W:].reshape(B, T, H_SB, D_SB))
            ks.append(h[Mp:, SB_W:2 * SB_W].reshape(Bs, Ts, H_SB, D_SB))
            vs.append(h[Mp:, 2 * SB_W:].reshape(Bs, Ts, H_SB, D_SB))
        x, xb = _ple(x, xb, pb[i], ple_proj, ple_gate, i, tm_n)

    return (x[:Mp].reshape(B, T, D), x[Mp:].reshape(Bs, Ts, D), jnp.stack(ret_p), jnp.stack(ret_s),
            jnp.stack(conv_p), jnp.stack(conv_s), jnp.stack(kp), jnp.stack(vp), jnp.stack(ks), jnp.stack(vs))
```

```python
import functools
import math

import jax
import jax.numpy as jnp
from jax import lax
from jax.experimental import pallas as pl
from jax.experimental.pallas import tpu as pltpu

F32 = jnp.float32
BF16 = jnp.bfloat16

DEPTH = 4
H_RET = 8
DK_RET = 256
DV_RET = 256
RET_W = H_RET * DV_RET
RET_CHUNK = 128
ROPE_BASE = 10000.0
CONV_K = 31
H_SB = 32
D_SB = 128
SB_W = H_SB * D_SB
TOP_K = 2
ALPHA = (2.0 * DEPTH) ** 0.25
LN_EPS = 1e-5

V7X_VMEM_LIMIT_BYTES = 56 * 1024 * 1024
LANES = 128
BF16_SUBLANES = 16

MOE_ROW_TILE = 512
MOE_GATHER_TILE = 256
MOE_COMBINE_TILE = 192
SB_Q_TILE = 256
SB_HEADS_PER_STEP = 2
SB_PAGES_PER_STEP = 4
SB_HEAD_GROUP = 8
SB_Q_PAD = 16


def _cparams(*sem):
    return pltpu.CompilerParams(dimension_semantics=sem, vmem_limit_bytes=V7X_VMEM_LIMIT_BYTES)


def _div_tile(n, cap, mult):
    best = None
    for t in range(mult, min(n, cap) + 1, mult):
        if n % t == 0:
            best = t
    assert best is not None, (n, cap, mult)
    return best


def _sigmoid(x):
    return 1.0 / (1.0 + jnp.exp(-x))


def _silu(x):
    return x * _sigmoid(x)


def _layer_norm(y, g, b):
    mu = jnp.mean(y, axis=-1, keepdims=True)
    yc = y - mu
    var = jnp.mean(yc * yc, axis=-1, keepdims=True)
    return yc * lax.rsqrt(var + LN_EPS) * g + b


def _cast_panel(w_ref, wb_ref, kc):
    def body(c, carry):
        r = pl.multiple_of(c * kc, kc)
        wb_ref[pl.ds(r, kc), :] = w_ref[pl.ds(r, kc), :].astype(BF16)
        return carry
    lax.fori_loop(0, w_ref.shape[0] // kc, body, 0)


def _mm_kernel(gid_ref, nv_ref, x_ref, *rest, n_w, n_extra, epi, kc, cast):
    w_refs = rest[:n_w]
    extras = rest[n_w:n_w + n_extra]
    i = pl.program_id(1)
    if cast:
        outs = rest[n_w + n_extra:len(rest) - n_w]
        wbs = rest[len(rest) - n_w:]
        changed = jnp.logical_or(i == 0, gid_ref[i] != gid_ref[jnp.maximum(i - 1, 0)])

        @pl.when(changed)
        def _():
            for w_ref, wb in zip(w_refs, wbs):
                _cast_panel(w_ref, wb, kc)
    else:
        outs = rest[n_w + n_extra:]
        wbs = w_refs

    @pl.when(i < nv_ref[0])
    def _():
        x = x_ref[...]
        accs = [jnp.dot(x, wb[...], preferred_element_type=F32) for wb in wbs]
        epi(accs, extras, outs)

    @pl.when(i >= nv_ref[0])
    def _():
        for o in outs:
            o[...] = jnp.zeros(o.shape, o.dtype)


def _epi_store(accs, extras, outs):
    outs[0][...] = accs[0].astype(outs[0].dtype)


def _epi_swiglu(accs, extras, outs):
    outs[0][...] = (_silu(accs[0]) * accs[1]).astype(outs[0].dtype)


def _epi_rowscale(accs, extras, outs):
    outs[0][...] = accs[0] * extras[0][...]


def _epi_ple(accs, extras, outs):
    xres, p_ref, wp_ref = extras
    pp = jnp.dot(p_ref[...], wp_ref[...].astype(BF16), preferred_element_type=F32)
    y = xres[...] + pp * _sigmoid(accs[0])
    outs[0][...] = y
    outs[1][...] = y.astype(BF16)


def _matmul(x, ws, gbase, *, name, tn, tm, epi, out_dtypes, extras=(), gid=None, n_valid=None,
            single_buffer_w=False):
    M, K = x.shape
    N = ws[0].shape[2]
    assert all(w.shape[1] == K for w in ws)
    assert M % tm == 0 and N % tn == 0
    n_i = M // tm
    if gid is None:
        gid = jnp.zeros((n_i,), jnp.int32)
        n_valid = jnp.full((1,), n_i, jnp.int32)
    cast = ws[0].dtype != BF16
    kc = _div_tile(K, 1024, BF16_SUBLANES)
    w_mode = dict(pipeline_mode=pl.Buffered(1)) if single_buffer_w else {}
    in_specs = [pl.BlockSpec((tm, K), lambda j, i, g, nv: (i, 0))]
    for _ in ws:
        in_specs.append(pl.BlockSpec((None, K, tn), lambda j, i, g, nv: (gbase + g[i], 0, j), **w_mode))
    for _, bs, im in extras:
        in_specs.append(pl.BlockSpec(bs, im))
    out_shape = [jax.ShapeDtypeStruct((M, N), dt) for dt in out_dtypes]
    out_specs = [pl.BlockSpec((tm, tn), lambda j, i, g, nv: (i, j)) for _ in out_dtypes]
    kern = functools.partial(_mm_kernel, n_w=len(ws), n_extra=len(extras), epi=epi, kc=kc, cast=cast)
    res = pl.pallas_call(
        kern,
        grid_spec=pltpu.PrefetchScalarGridSpec(
            num_scalar_prefetch=2, grid=(N // tn, n_i), in_specs=in_specs, out_specs=out_specs,
            scratch_shapes=[pltpu.VMEM((K, tn), BF16) for _ in ws] if cast else []),
        out_shape=out_shape, name=name,
        compiler_params=_cparams("arbitrary", "arbitrary"),
    )(gid, n_valid, x, *ws, *[e[0] for e in extras])
    return res


def _post_norm_kernel(x_ref, s_ref, g_ref, b_ref, o_ref, ob_ref):
    y = _layer_norm(ALPHA * x_ref[...] + s_ref[...], g_ref[...], b_ref[...])
    o_ref[...] = y
    ob_ref[...] = y.astype(BF16)


def _post_norm(x, sub, g, b, layer):
    M, D = x.shape
    tm = _div_tile(M, 256, BF16_SUBLANES)
    g3 = g.reshape(g.shape[0], 1, D)
    b3 = b.reshape(b.shape[0], 1, D)
    row = pl.BlockSpec((tm, D), lambda i: (i, 0))
    vec = pl.BlockSpec((None, 1, D), lambda i: (layer, 0, 0))
    return pl.pallas_call(
        _post_norm_kernel, grid=(M // tm,), in_specs=[row, row, vec, vec], out_specs=[row, row],
        out_shape=[jax.ShapeDtypeStruct((M, D), F32), jax.ShapeDtypeStruct((M, D), BF16)],
        name="post_norm", compiler_params=_cparams("arbitrary"),
    )(x, sub, g3, b3)


def _norm_silu_kernel(c_ref, g_ref, b_ref, o_ref):
    o_ref[...] = _silu(_layer_norm(c_ref[...], g_ref[...], b_ref[...])).astype(BF16)


def _norm_silu(c, g, b, layer):
    M, D = c.shape
    tm = _div_tile(M, 512, BF16_SUBLANES)
    g3 = g.reshape(g.shape[0], 1, D)
    b3 = b.reshape(b.shape[0], 1, D)
    row = pl.BlockSpec((tm, D), lambda i: (i, 0))
    vec = pl.BlockSpec((None, 1, D), lambda i: (layer, 0, 0))
    return pl.pallas_call(
        _norm_silu_kernel, grid=(M // tm,), in_specs=[row, vec, vec], out_specs=row,
        out_shape=jax.ShapeDtypeStruct((M, D), BF16), name="conv_norm_silu",
        compiler_params=_cparams("arbitrary"),
    )(c, g3, b3)


def _retention_kernel(q_ref, k_ref, v_ref, g_ref, cos_ref, sin_ref, dec_ref, qs_ref, ks_ref, rs_ref,
                      gg_ref, gb_ref, r0_ref, o_ref, st_ref, r_scr, *, n_chunks):
    c = pl.program_id(2)

    @pl.when(c == 0)
    def _():
        r_scr[...] = r0_ref[...]

    cos = cos_ref[...]
    sin = sin_ref[...]
    half = DK_RET // 2

    def rot(x):
        x1 = x[:, :half]
        x2 = x[:, half:]
        return jnp.concatenate([x1 * cos - x2 * sin, x1 * sin + x2 * cos], axis=-1)

    qr = rot(q_ref[...])
    kr = rot(k_ref[...]) * (DK_RET ** -0.5)
    qb = qr.astype(BF16)
    kb = kr.astype(BF16)
    vb = v_ref[...].astype(BF16)
    s = lax.dot_general(qb, kb, (((1,), (1,)), ((), ())), preferred_element_type=F32) * dec_ref[...]
    inner = jnp.dot(s.astype(BF16), vb, preferred_element_type=F32)
    r = r_scr[...]
    cross = jnp.dot(qb, r.astype(BF16), preferred_element_type=F32) * qs_ref[...]
    ksc = (kr * ks_ref[...]).astype(BF16)
    r_new = r * rs_ref[...] + lax.dot_general(ksc, vb, (((0,), (0,)), ((), ())), preferred_element_type=F32)
    r_scr[...] = r_new
    y = _layer_norm(inner + cross, gg_ref[...], gb_ref[...])
    o_ref[...] = (_silu(g_ref[...]) * y).astype(o_ref.dtype)

    @pl.when(c == n_chunks - 1)
    def _():
        st_ref[...] = r_new


def _retention(h, row_blk0, n_batch, n_chunks, cos, sin, decay, qsc, ksc, rsc, gn_g, gn_b, r0, layer):
    C = RET_CHUNK
    hq = DK_RET
    ncb = RET_W // hq

    def rows(col0):
        return pl.BlockSpec((C, hq), lambda b, hh, c: (row_blk0 + b * n_chunks + c, col0 + hh))

    tab = pl.BlockSpec((C, DK_RET // 2), lambda b, hh, c: (c, 0))
    per_head = lambda shp: pl.BlockSpec((None,) + shp, lambda b, hh, c: (hh, 0, 0))
    gvec = pl.BlockSpec((None, 1, hq), lambda b, hh, c: (layer, 0, hh))
    state = pl.BlockSpec((None, None, DK_RET, DV_RET), lambda b, hh, c: (b, hh, 0, 0))
    out_rows = n_batch * n_chunks * C
    return pl.pallas_call(
        functools.partial(_retention_kernel, n_chunks=n_chunks),
        grid=(n_batch, H_RET, n_chunks),
        in_specs=[rows(0), rows(ncb), rows(2 * ncb), rows(3 * ncb), tab, tab,
                  per_head((C, C)), per_head((C, 1)), per_head((C, 1)), per_head((1, DV_RET)),
                  gvec, gvec, state],
        out_specs=[pl.BlockSpec((C, hq), lambda b, hh, c: (b * n_chunks + c, hh)), state],
        out_shape=[jax.ShapeDtypeStruct((out_rows, RET_W), BF16),
                   jax.ShapeDtypeStruct((n_batch, H_RET, DK_RET, DV_RET), F32)],
        scratch_shapes=[pltpu.VMEM((DK_RET, DV_RET), F32)], name="retention",
        compiler_params=_cparams("arbitrary", "arbitrary", "arbitrary"),
    )(h, h, h, h, cos, sin, decay, qsc, ksc, rsc,
      gn_g.reshape(gn_g.shape[0], 1, RET_W), gn_b.reshape(gn_b.shape[0], 1, RET_W), r0)


def _retention_tables(chunk, pad_to):
    log_gamma = jnp.log1p(-jnp.exp2(-5.0 - jnp.arange(H_RET, dtype=F32)))
    idx = jnp.arange(chunk, dtype=F32)
    lg = log_gamma[:, None]
    diff = idx[:, None] - idx[None, :]
    decay = jnp.where(diff >= 0, jnp.exp(lg[:, :, None] * jnp.maximum(diff, 0.0)), 0.0)
    qsc = jnp.exp(lg * (idx + 1.0))[:, :, None]
    ksc = jnp.exp(lg * (chunk - 1.0 - idx))[:, :, None]
    rsc = jnp.broadcast_to(jnp.exp(log_gamma * chunk)[:, None, None], (H_RET, 1, DV_RET))
    p = pad_to - chunk
    decay = jnp.pad(decay, ((0, 0), (0, p), (0, p)))
    qsc = jnp.pad(qsc, ((0, 0), (0, p), (0, 0)))
    ksc = jnp.pad(ksc, ((0, 0), (0, p), (0, 0)))
    return decay, qsc, ksc, rsc


def _rope_tables(pos, pad_to):
    half = DK_RET // 2
    inv = jnp.exp(-math.log(ROPE_BASE) * jnp.arange(half, dtype=F32) / half)
    ang = pos.astype(F32)[:, None] * inv[None, :]
    p = pad_to - pos.shape[0]
    return jnp.pad(jnp.cos(ang), ((0, p), (0, 0))), jnp.pad(jnp.sin(ang), ((0, p), (0, 0)))


CONV_HIST = 32
CONV_LANES = 512
CONV_ROWS = 64


def _conv_kernel(a_ref, b_ref, st_ref, w_ref, bias_ref, o_ref, buf_ref, full, *, tt, n_t):
    t = pl.program_id(2)
    kh = CONV_K - 1
    off = CONV_HIST - kh

    @pl.when(t == 0)
    def _():
        full[0:off, :] = jnp.zeros((off, CONV_LANES), F32)
        full[off:CONV_HIST, :] = st_ref[...]

    @pl.when(t > 0)
    def _():
        full[0:CONV_HIST, :] = full[tt:tt + CONV_HIST, :]

    full[CONV_HIST:CONV_HIST + tt, :] = a_ref[...] * _sigmoid(b_ref[...])
    rc = min(CONV_ROWS, tt)
    for r0 in range(0, tt, rc):
        acc = jnp.zeros((rc, CONV_LANES), F32) + bias_ref[...]
        for j in range(CONV_K):
            acc = acc + w_ref[j:j + 1, :] * full[off + r0 + j:off + r0 + j + rc, :]
        o_ref[r0:r0 + rc, :] = acc

    @pl.when(t == n_t - 1)
    def _():
        buf_ref[...] = full[CONV_HIST + tt - kh:CONV_HIST + tt, :]


def _conv(h, row_blk0, n_batch, n_t, tt, col_a, col_b, state, w, bias, layer):
    W = w.shape[2]
    n_l = W // CONV_LANES
    kh = CONV_K - 1
    rows = lambda col0: pl.BlockSpec((tt, CONV_LANES), lambda b, l, t: (row_blk0 + b * n_t + t, col0 + l))
    return pl.pallas_call(
        functools.partial(_conv_kernel, tt=tt, n_t=n_t),
        grid=(n_batch, n_l, n_t),
        in_specs=[rows(col_a), rows(col_b),
                  pl.BlockSpec((None, kh, CONV_LANES), lambda b, l, t: (b, 0, l)),
                  pl.BlockSpec((None, CONV_K, CONV_LANES), lambda b, l, t: (layer, 0, l)),
                  pl.BlockSpec((None, 1, CONV_LANES), lambda b, l, t: (layer, 0, l))],
        out_specs=[pl.BlockSpec((tt, CONV_LANES), lambda b, l, t: (b * n_t + t, l)),
                   pl.BlockSpec((None, kh, CONV_LANES), lambda b, l, t: (b, 0, l))],
        out_shape=[jax.ShapeDtypeStruct((n_batch * n_t * tt, W), F32),
                   jax.ShapeDtypeStruct((n_batch, kh, W), F32)],
        scratch_shapes=[pltpu.VMEM((CONV_HIST + tt, CONV_LANES), F32)], name="glu_conv",
        compiler_params=_cparams("arbitrary", "arbitrary", "arbitrary"),
    )(h, h, state, w, bias.reshape(bias.shape[0], 1, W))


def _sb_block(z, mask, carry, tri):
    w = jnp.maximum(z, 0.0) + jnp.log(1.0 + jnp.exp(-jnp.abs(z)))
    if mask is not None:
        w = jnp.where(mask, w, 0.0)
    w_hi = w.astype(BF16)
    w_lo = (w - w_hi.astype(F32)).astype(BF16)
    tail = jnp.dot(w_hi, tri, preferred_element_type=F32) + jnp.dot(w_lo, tri, preferred_element_type=F32)
    a = jnp.exp(z - tail - carry)
    if mask is not None:
        a = jnp.where(mask, a, 0.0)
    return a, carry + tail[:, :1]


def _tri(n):
    r = jnp.arange(n)
    return (r[:, None] >= r[None, :]).astype(BF16)


def _sb_prompt_kernel(bias_ref, q_ref, k_ref, v_ref, tri_ref, o_ref, *, tq):
    nh = SB_HEADS_PER_STEP
    hp = pl.program_id(1)
    qi = pl.program_id(2)
    scale = D_SB ** -0.5
    tri = tri_ref[...]
    cols = [slice(g * D_SB, (g + 1) * D_SB) for g in range(nh)]
    qbs = [q_ref[:, cols[g]].astype(BF16) for g in range(nh)]
    biases = [bias_ref[hp * nh + g] for g in range(nh)]

    def key_block(kb):
        return pl.multiple_of(jnp.maximum(kb, 0) * tq, tq)

    def logits(r0):
        return tuple(lax.dot_general(qbs[g], k_ref[pl.ds(r0, tq), cols[g]].astype(BF16), (((1,), (1,)), ((), ())),
                                     preferred_element_type=F32) for g in range(nh))

    def visit(r0, zs, mask, state):
        new = []
        for g in range(nh):
            carry, o = state[g]
            vb = v_ref[pl.ds(r0, tq), cols[g]].astype(BF16)
            a, carry = _sb_block(zs[g] * scale + biases[g], mask, carry, tri)
            new.append((carry, o + jnp.dot(a.astype(BF16), vb, preferred_element_type=F32)))
        return tuple(new)

    row = lax.broadcasted_iota(jnp.int32, (tq, tq), 0)
    col = lax.broadcasted_iota(jnp.int32, (tq, tq), 1)
    state = tuple((jnp.zeros((tq, 1), F32), jnp.zeros((tq, D_SB), F32)) for _ in range(nh))
    zs_next = logits(key_block(qi - 1))
    state = visit(key_block(qi), logits(key_block(qi)), col < row, state)

    def body(kk, st):
        zs, state = st
        return logits(key_block(qi - kk - 1)), visit(key_block(qi - kk), zs, None, state)

    _, state = lax.fori_loop(1, qi + 1, body, (zs_next, state))
    for g in range(nh):
        o_ref[:, cols[g]] = state[g][1].astype(BF16)


def _sb_prompt(h, bias, n_batch, seq):
    tq = min(SB_Q_TILE, seq)
    nq = seq // tq
    wide = SB_HEADS_PER_STEP * D_SB
    n_hp = H_SB // SB_HEADS_PER_STEP
    return pl.pallas_call(
        functools.partial(_sb_prompt_kernel, tq=tq),
        grid_spec=pltpu.PrefetchScalarGridSpec(
            num_scalar_prefetch=1, grid=(n_batch, n_hp, nq),
            in_specs=[pl.BlockSpec((tq, wide), lambda b, hp, qi, bias: (b * nq + qi, hp)),
                      pl.BlockSpec((seq, wide), lambda b, hp, qi, bias: (b, n_hp + hp)),
                      pl.BlockSpec((seq, wide), lambda b, hp, qi, bias: (b, 2 * n_hp + hp)),
                      pl.BlockSpec((tq, tq), lambda b, hp, qi, bias: (0, 0))],
            out_specs=pl.BlockSpec((tq, wide), lambda b, hp, qi, bias: (b * nq + qi, hp))),
        out_shape=jax.ShapeDtypeStruct((n_batch * seq, SB_W), BF16), name="sb_prompt",
        compiler_params=_cparams("arbitrary", "arbitrary", "arbitrary"),
    )(bias, h, h, h, _tri(tq))


def _sb_sample_kernel(pt_ref, bias_ref, q_ref, kn_ref, vn_ref, *rest, n_steps, t_dec, page):
    P = SB_PAGES_PER_STEP
    HG = SB_HEAD_GROUP
    k_pages = rest[:P]
    v_pages = rest[P:2 * P]
    tri_ref, o_ref, qs, carry, acc = rest[2 * P:]
    hg = pl.program_id(1)
    s = pl.program_id(2)
    scale = D_SB ** -0.5
    QP = SB_Q_PAD
    rows = HG * QP
    lanes = lambda j: slice(j * D_SB, (j + 1) * D_SB)

    def bias_col():
        hrow = lax.broadcasted_iota(jnp.int32, (rows, 1), 0) // QP
        out = jnp.zeros((rows, 1), F32)
        for j in range(HG):
            out = jnp.where(hrow == j, bias_ref[hg * HG + j], out)
        return out

    def process(kcat, vcat, mask, tri):
        z = jnp.concatenate(
            [lax.dot_general(qs[j], kcat[j], (((1,), (1,)), ((), ())), preferred_element_type=F32)
             for j in range(HG)], axis=0)
        a, new_carry = _sb_block(z * scale + bias_col(), mask, carry[...], tri)
        carry[...] = new_carry
        ab = a.astype(BF16)
        for j in range(HG):
            acc[j * QP:(j + 1) * QP, :] += jnp.dot(ab[j * QP:(j + 1) * QP, :], vcat[j], preferred_element_type=F32)

    @pl.when(s == 0)
    def _():
        carry[...] = jnp.zeros_like(carry)
        acc[...] = jnp.zeros_like(acc)
        pad = jnp.zeros((QP - t_dec, D_SB), F32)
        for j in range(HG):
            qs[j] = jnp.concatenate([q_ref[:, lanes(j)], pad], axis=0).astype(BF16)
        kpad = jnp.zeros((page - t_dec, D_SB), F32)
        kcat = [jnp.concatenate([kn_ref[:, lanes(j)], kpad], axis=0).astype(BF16) for j in range(HG)]
        vcat = [jnp.concatenate([vn_ref[:, lanes(j)], kpad], axis=0).astype(BF16) for j in range(HG)]
        trow = lax.broadcasted_iota(jnp.int32, (rows, page), 0) % QP
        col = lax.broadcasted_iota(jnp.int32, (rows, page), 1)
        process(kcat, vcat, col < trow, tri_ref[0:page, 0:page])

    @pl.when(s > 0)
    def _():
        kts = [pltpu.einshape("khd->hkd", kp[...]) for kp in k_pages]
        vts = [pltpu.einshape("khd->hkd", vp[...]) for vp in v_pages]
        kcat = [jnp.concatenate([kt[j] for kt in kts], axis=0).astype(BF16) for j in range(HG)]
        vcat = [jnp.concatenate([vt[j] for vt in vts], axis=0).astype(BF16) for j in range(HG)]
        process(kcat, vcat, None, tri_ref[...])

    @pl.when(s == n_steps - 1)
    def _():
        for j in range(HG):
            o_ref[:, lanes(j)] = acc[j * QP:j * QP + t_dec, :]


def _sb_sample(h, row_blk0, cache_k, cache_v, page_table, bias, layer, n_batch, t_dec):
    P = SB_PAGES_PER_STEP
    HG = SB_HEAD_GROUP
    n_pages = page_table.shape[1]
    page = cache_k.shape[2]
    assert n_pages % P == 0 and t_dec <= SB_Q_PAD and t_dec <= page and H_SB % HG == 0
    n_groups = n_pages // P
    n_steps = n_groups + 1
    n_hg = H_SB // HG
    wide = HG * D_SB

    def page_spec(i):
        def im(b, hg, s, pt, bias):
            g = n_groups - jnp.maximum(s, 1)
            return (layer, pt[b * n_pages + g * P + i], 0, hg, 0)
        return pl.BlockSpec((None, None, page, HG, D_SB), im)

    new = lambda part: pl.BlockSpec((t_dec, wide), lambda b, hg, s, pt, bias: (row_blk0 + b, part * n_hg + hg))
    return pl.pallas_call(
        functools.partial(_sb_sample_kernel, n_steps=n_steps, t_dec=t_dec, page=page),
        grid_spec=pltpu.PrefetchScalarGridSpec(
            num_scalar_prefetch=2, grid=(n_batch, n_hg, n_steps),
            in_specs=[new(0), new(1), new(2)] + [page_spec(i) for i in range(P)] * 2
            + [pl.BlockSpec((P * page, P * page), lambda b, hg, s, pt, bias: (0, 0))],
            out_specs=pl.BlockSpec((t_dec, wide), lambda b, hg, s, pt, bias: (b, hg)),
            scratch_shapes=[pltpu.VMEM((HG, SB_Q_PAD, D_SB), BF16),
                            pltpu.VMEM((HG * SB_Q_PAD, 1), F32),
                            pltpu.VMEM((HG * SB_Q_PAD, D_SB), F32)]),
        out_shape=jax.ShapeDtypeStruct((n_batch * t_dec, SB_W), F32), name="sb_sample",
        compiler_params=_cparams("arbitrary", "arbitrary", "arbitrary"),
    )(page_table.reshape(-1), bias, h, h, h, *([cache_k] * P), *([cache_v] * P), _tri(P * page))


def _split_bf16(x):
    hi = x.astype(BF16)
    return hi, (x - hi.astype(F32)).astype(BF16)


def _router_kernel(x_ref, w_ref, g_ref, e_ref, *, n_exp):
    xh, xl = _split_bf16(x_ref[...])
    wh, wl = _split_bf16(w_ref[...])
    dot = functools.partial(jnp.dot, preferred_element_type=F32)
    logits = dot(xh, wh) + dot(xl, wh) + dot(xh, wl)
    lane = lax.broadcasted_iota(jnp.int32, logits.shape, 1)
    lanef = lane.astype(F32)
    neg = jnp.float32(-jnp.inf)
    logits = jnp.where(lane < n_exp, logits, neg)
    m1 = jnp.max(logits, axis=-1, keepdims=True)
    i1 = jnp.min(jnp.where(logits == m1, lanef, float(LANES)), axis=-1, keepdims=True)
    rest = jnp.where(lanef == i1, neg, logits)
    m2 = jnp.max(rest, axis=-1, keepdims=True)
    i2 = jnp.min(jnp.where(rest == m2, lanef, float(LANES)), axis=-1, keepdims=True)
    e2 = jnp.exp(m2 - m1)
    den = 1.0 + e2
    g_ref[...] = jnp.where(lane == 0, 1.0 / den, jnp.where(lane == 1, e2 / den, 0.0))
    e_ref[...] = jnp.where(lane == 0, i1, jnp.where(lane == 1, i2, 0.0)).astype(jnp.int32)


def _router(x, router_w, layer):
    M, D = x.shape
    n_exp = router_w.shape[2]
    tm = _div_tile(M, 256, 8)
    wpad = jnp.pad(router_w, ((0, 0), (0, 0), (0, LANES - n_exp)))
    row = lambda w: pl.BlockSpec((tm, w), lambda i: (i, 0))
    gates, eidx = pl.pallas_call(
        functools.partial(_router_kernel, n_exp=n_exp), grid=(M // tm,),
        in_specs=[row(D), pl.BlockSpec((None, D, LANES), lambda i: (layer, 0, 0))],
        out_specs=[row(LANES), row(LANES)],
        out_shape=[jax.ShapeDtypeStruct((M, LANES), F32), jax.ShapeDtypeStruct((M, LANES), jnp.int32)],
        name="moe_router", compiler_params=_cparams("arbitrary"),
    )(x, wpad)
    return gates[:, :TOP_K], eidx[:, :TOP_K]


def _row_copy(src_hbm, dst, src_row, dst_row, sem):
    return pltpu.make_async_copy(src_hbm.at[pl.ds(src_row, 1), :], dst.at[pl.ds(dst_row, 1), :], sem)


def _gather_kernel(tok_ref, x_hbm, o_ref, buf, sem, *, tg):
    i = pl.program_id(0)

    def issue(r, carry):
        _row_copy(x_hbm, buf, tok_ref[i * tg + r], r, sem).start()
        return carry

    def wait(r, carry):
        _row_copy(x_hbm, buf, 0, r, sem).wait()
        return carry

    lax.fori_loop(0, tg, issue, 0)
    lax.fori_loop(0, tg, wait, 0)
    o_ref[...] = buf[...].astype(BF16)


def _gather_rows(x, row_token, tg):
    R = row_token.shape[0]
    D = x.shape[1]
    return pl.pallas_call(
        functools.partial(_gather_kernel, tg=tg),
        grid_spec=pltpu.PrefetchScalarGridSpec(
            num_scalar_prefetch=1, grid=(R // tg,),
            in_specs=[pl.BlockSpec(memory_space=pl.ANY)],
            out_specs=pl.BlockSpec((tg, D), lambda i, tok: (i, 0)),
            scratch_shapes=[pltpu.VMEM((tg, D), F32), pltpu.SemaphoreType.DMA(())]),
        out_shape=jax.ShapeDtypeStruct((R, D), BF16), name="moe_gather",
        compiler_params=_cparams("arbitrary"),
    )(row_token, x)


def _combine_norm_kernel(dest_ref, x_ref, g_ref, b_ref, y_hbm, o_ref, ob_ref, buf, sem, *, tc):
    i = pl.program_id(0)

    def issue(r, carry):
        for k in range(TOP_K):
            _row_copy(y_hbm, buf.at[k], dest_ref[(i * tc + r) * TOP_K + k], r, sem).start()
        return carry

    def wait(r, carry):
        for k in range(TOP_K):
            _row_copy(y_hbm, buf.at[k], 0, r, sem).wait()
        return carry

    lax.fori_loop(0, tc, issue, 0)
    lax.fori_loop(0, tc, wait, 0)
    moe = buf[0] + buf[1]
    y = _layer_norm(ALPHA * x_ref[...] + moe, g_ref[...], b_ref[...])
    o_ref[...] = y
    ob_ref[...] = y.astype(BF16)


def _combine_norm(x, y_rows, dest, g, b, layer):
    M, D = x.shape
    tc = _div_tile(M, MOE_COMBINE_TILE, BF16_SUBLANES)
    g3 = g.reshape(g.shape[0], 1, D)
    b3 = b.reshape(b.shape[0], 1, D)
    row = pl.BlockSpec((tc, D), lambda i, d: (i, 0))
    vec = pl.BlockSpec((None, 1, D), lambda i, d: (layer, 0, 0))
    return pl.pallas_call(
        functools.partial(_combine_norm_kernel, tc=tc),
        grid_spec=pltpu.PrefetchScalarGridSpec(
            num_scalar_prefetch=1, grid=(M // tc,),
            in_specs=[row, vec, vec, pl.BlockSpec(memory_space=pl.ANY)],
            out_specs=[row, row],
            scratch_shapes=[pltpu.VMEM((TOP_K, tc, D), F32), pltpu.SemaphoreType.DMA(())]),
        out_shape=[jax.ShapeDtypeStruct((M, D), F32), jax.ShapeDtypeStruct((M, D), BF16)],
        name="moe_combine_norm", compiler_params=_cparams("arbitrary"),
    )(dest.reshape(-1), x, g3, b3, y_rows)


def _moe_plan(eidx, gates, n_exp, tile):
    M = eidx.shape[0]
    n_tiles = (M * TOP_K + n_exp * (tile - 1)) // tile
    R = n_tiles * tile
    onehot = (eidx[:, :, None] == jnp.arange(n_exp, dtype=jnp.int32)[None, None, :]).astype(jnp.int32)
    per_tok = onehot.sum(axis=1)
    counts = per_tok.sum(axis=0)
    pos = jnp.cumsum(per_tok, axis=0) - per_tok
    padded = ((counts + tile - 1) // tile) * tile
    ends = jnp.cumsum(padded)
    offs = ends - padded
    dest = jnp.take_along_axis(offs[None, :] + pos, eidx, axis=1)
    tok = jnp.broadcast_to(jnp.arange(M, dtype=jnp.int32)[:, None], (M, TOP_K))
    row_token = jnp.zeros((R,), jnp.int32).at[dest.reshape(-1)].set(tok.reshape(-1))
    row_gate = jnp.zeros((R,), F32).at[dest.reshape(-1)].set(gates.reshape(-1))
    tile_start = jnp.arange(n_tiles, dtype=jnp.int32) * tile
    tile_exp = jnp.minimum(jnp.searchsorted(ends, tile_start, side="right"), n_exp - 1).astype(jnp.int32)
    n_valid = (ends[-1] // tile).astype(jnp.int32).reshape(1)
    last = tile_exp[jnp.maximum(n_valid[0] - 1, 0)]
    tile_exp = jnp.where(tile_start < ends[-1], tile_exp, last)
    return row_token, row_gate.reshape(R, 1), dest.astype(jnp.int32), tile_exp, n_valid


def _moe_norm(x, router_w, w1, w3, w2, g, b, layer, moe_layer):
    n_exp = router_w.shape[2]
    gates, eidx = _router(x, router_w, moe_layer)
    row_token, row_gate, dest, tile_exp, n_valid = _moe_plan(eidx, gates, n_exp, MOE_ROW_TILE)
    xs = _gather_rows(x, row_token, MOE_GATHER_TILE)
    w1f = w1.reshape((-1,) + w1.shape[2:])
    w3f = w3.reshape((-1,) + w3.shape[2:])
    w2f = w2.reshape((-1,) + w2.shape[2:])
    gb = moe_layer * n_exp
    (hs,) = _matmul(xs, [w1f, w3f], gb, name="moe_up", tn=512, tm=MOE_ROW_TILE, epi=_epi_swiglu, out_dtypes=[BF16],
                    gid=tile_exp, n_valid=n_valid)
    gate_spec = (row_gate, (MOE_ROW_TILE, 1), lambda j, i, gid, nv: (i, 0))
    (ys,) = _matmul(hs, [w2f], gb, name="moe_down", tn=512, tm=MOE_ROW_TILE, epi=_epi_rowscale, out_dtypes=[F32],
                    extras=[gate_spec], gid=tile_exp, n_valid=n_valid)
    return _combine_norm(x, ys, dest, g, b, layer)


def _ple(x, xb, pb, ple_proj, ple_gate, layer, tm):
    tn = 512
    extras = [(x, (tm, tn), lambda j, i, g, nv: (i, j)),
              (pb, (tm, pb.shape[1]), lambda j, i, g, nv: (i, 0)),
              (ple_proj, (None, ple_proj.shape[1], tn), lambda j, i, g, nv: (layer, 0, j))]
    return _matmul(xb, [ple_gate], layer, name="ple", tn=tn, tm=tm, epi=_epi_ple, out_dtypes=[F32, BF16],
                   extras=extras, single_buffer_w=True)


def kernel(x_prompt, x_sample, p_prompt, p_sample, state_ret, state_conv, cache_k, cache_v, page_table,
           ln_mix_g, ln_mix_b, ln_ffn_g, ln_ffn_b, ple_proj, ple_gate, ev_w_in, ev_w_out, ret_gn_g, ret_gn_b,
           conv_w, conv_b, conv_ln_g, conv_ln_b, ffn_w1, ffn_w3, ffn_w2, sb_w_qkv, sb_w_out, sb_bias,
           moe_router, moe_w1, moe_w3, moe_w2):
    B, T, D = x_prompt.shape
    Bs, Ts, _ = x_sample.shape
    Mp, Ms = B * T, Bs * Ts
    M = Mp + Ms
    past = page_table.shape[1] * cache_k.shape[2]
    conv_width = conv_w.shape[2]
    assert T % RET_CHUNK == 0 and Ts <= RET_CHUNK and Mp % RET_CHUNK == 0 and Mp % Ts == 0
    assert conv_width % CONV_LANES == 0 and (4 * RET_W) % CONV_LANES == 0

    tm = _div_tile(M, 1376, BF16_SUBLANES)
    tm_n = _div_tile(M, 688, BF16_SUBLANES)
    ffn_w2_bf16 = ffn_w2.astype(BF16)
    x = jnp.concatenate([x_prompt.reshape(Mp, D), x_sample.reshape(Ms, D)], axis=0)
    xb = x.astype(BF16)
    pb = jnp.concatenate([p_prompt.reshape(DEPTH, Mp, -1), p_sample.reshape(DEPTH, Ms, -1)], axis=1).astype(BF16)

    nc = T // RET_CHUNK
    cos_p, sin_p = _rope_tables(jnp.arange(T, dtype=jnp.int32), T)
    cos_s, sin_s = _rope_tables(past + jnp.arange(Ts, dtype=jnp.int32), RET_CHUNK)
    tab_p = _retention_tables(RET_CHUNK, RET_CHUNK)
    tab_s = _retention_tables(Ts, RET_CHUNK)
    zero_state = jnp.zeros((B, H_RET, DK_RET, DV_RET), F32)
    zero_buf = jnp.zeros((B, CONV_K - 1, conv_width), F32)
    conv_tt = min(128, T)

    ret_p, ret_s, conv_p, conv_s, kp, vp, ks, vs = [], [], [], [], [], [], [], []
    for i in range(DEPTH):
        if i % 2 == 0:
            e = i // 2
            (h,) = _matmul(xb, [ev_w_in], e, name="ev_in", tn=512, tm=tm, epi=_epi_store, out_dtypes=[F32])
            r_p, st_p = _retention(h, 0, B, nc, cos_p, sin_p, *tab_p, ret_gn_g, ret_gn_b, zero_state, e)
            hs_pad = jnp.pad(h[Mp:, :4 * RET_W].reshape(Bs, Ts, 4 * RET_W),
                             ((0, 0), (0, RET_CHUNK - Ts), (0, 0))).reshape(Bs * RET_CHUNK, 4 * RET_W)
            r_s, st_s = _retention(hs_pad, 0, Bs, 1, cos_s, sin_s, *tab_s, ret_gn_g, ret_gn_b, state_ret[e], e)
            r_s = r_s.reshape(Bs, RET_CHUNK, RET_W)[:, :Ts].reshape(Ms, RET_W)
            ca = (4 * RET_W) // CONV_LANES
            cb = ca + conv_width // CONV_LANES
            c_p, buf_p = _conv(h, 0, B, T // conv_tt, conv_tt, ca, cb, zero_buf, conv_w, conv_b, e)
            c_s, buf_s = _conv(h, Mp // Ts, Bs, 1, Ts, ca, cb, state_conv[e], conv_w, conv_b, e)
            cact = _norm_silu(jnp.concatenate([c_p, c_s], axis=0), conv_ln_g, conv_ln_b, e)
            mix = jnp.concatenate([jnp.concatenate([r_p, r_s], axis=0), cact], axis=1)
            (mo,) = _matmul(mix, [ev_w_out], e, name="ev_out", tn=512, tm=tm, epi=_epi_store, out_dtypes=[F32])
            x, xb = _post_norm(x, mo, ln_mix_g, ln_mix_b, i)
            (hf,) = _matmul(xb, [ffn_w1, ffn_w3], e, name="ffn_up", tn=256, tm=tm, epi=_epi_swiglu,
                            out_dtypes=[BF16])
            (f2,) = _matmul(hf, [ffn_w2_bf16], e, name="ffn_down", tn=512, tm=tm_n, epi=_epi_store, out_dtypes=[F32],
                            single_buffer_w=True)
            x, xb = _post_norm(x, f2, ln_ffn_g, ln_ffn_b, i)
            ret_p.append(st_p)
            ret_s.append(st_s)
            conv_p.append(buf_p)
            conv_s.append(buf_s)
        else:
            o = i // 2
            (h,) = _matmul(xb, [sb_w_qkv], o, name="sb_qkv", tn=512, tm=tm, epi=_epi_store, out_dtypes=[F32])
            a_p = _sb_prompt(h, sb_bias[o], B, T)
            a_s = _sb_sample(h, Mp // Ts, cache_k, cache_v, page_table, sb_bias[o], o, Bs, Ts)
            att = jnp.concatenate([a_p, a_s.astype(BF16)], axis=0)
            (mo,) = _matmul(att, [sb_w_out], o, name="sb_out", tn=512, tm=tm, epi=_epi_store, out_dtypes=[F32])
            x, xb = _post_norm(x, mo, ln_mix_g, ln_mix_b, i)
            x, xb = _moe_norm(x, moe_router, moe_w1, moe_w3, moe_w2, ln_ffn_g, ln_ffn_b, i, o)
            kp.append(h[:Mp, SB_W:2 * SB_W].reshape(B, T, H_SB, D_SB))
            vp.append(h[:Mp, 2 * SB_W:].reshape(B, T, H_SB, D_SB))
            ks.append(h[Mp:, SB_W:2 * SB_W].reshape(Bs, Ts, H_SB, D_SB))
            vs.append(h[Mp:, 2 * SB_W:].reshape(Bs, Ts, H_SB, D_SB))
        x, xb = _ple(x, xb, pb[i], ple_proj, ple_gate, i, tm_n)

    return (x[:Mp].reshape(B, T, D), x[Mp:].reshape(Bs, Ts, D), jnp.stack(ret_p), jnp.stack(ret_s),
            jnp.stack(conv_p), jnp.stack(conv_s), jnp.stack(kp), jnp.stack(vp), jnp.stack(ks), jnp.stack(vs))
```

```python
import functools
import math

import jax
import jax.numpy as jnp
from jax import lax
from jax.experimental import pallas as pl
from jax.experimental.pallas import tpu as pltpu

F32 = jnp.float32
BF16 = jnp.bfloat16

DEPTH = 4
H_RET = 8
DK_RET = 256
DV_RET = 256
RET_W = H_RET * DV_RET
RET_CHUNK = 128
RET_HEADS_PER_STEP = 4
ROPE_BASE = 10000.0
CONV_K = 31
H_SB = 32
D_SB = 128
SB_W = H_SB * D_SB
TOP_K = 2
ALPHA = (2.0 * DEPTH) ** 0.25
LN_EPS = 1e-5

V7X_VMEM_LIMIT_BYTES = 56 * 1024 * 1024
LANES = 128
BF16_SUBLANES = 16

MOE_ROW_TILE = 512
MOE_GATHER_TILE = 256
MOE_COMBINE_TILE = 192
SB_Q_TILE = 256
SB_HEADS_PER_STEP = 2
SB_PAGES_PER_STEP = 4
SB_HEAD_GROUP = 8
SB_Q_PAD = 16


def _cparams(*sem):
    return pltpu.CompilerParams(dimension_semantics=sem, vmem_limit_bytes=V7X_VMEM_LIMIT_BYTES)


def _div_tile(n, cap, mult):
    best = None
    for t in range(mult, min(n, cap) + 1, mult):
        if n % t == 0:
            best = t
    assert best is not None, (n, cap, mult)
    return best


def _sigmoid(x):
    return 1.0 / (1.0 + jnp.exp(-x))


def _silu(x):
    return x * _sigmoid(x)


def _layer_norm(y, g, b):
    mu = jnp.mean(y, axis=-1, keepdims=True)
    yc = y - mu
    var = jnp.mean(yc * yc, axis=-1, keepdims=True)
    return yc * lax.rsqrt(var + LN_EPS) * g + b


def _cast_panel(w_ref, wb_ref, kc):
    def body(c, carry):
        r = pl.multiple_of(c * kc, kc)
        wb_ref[pl.ds(r, kc), :] = w_ref[pl.ds(r, kc), :].astype(BF16)
        return carry
    lax.fori_loop(0, w_ref.shape[0] // kc, body, 0)


def _mm_kernel(gid_ref, nv_ref, x_ref, *rest, n_w, n_extra, epi, kc, cast):
    w_refs = rest[:n_w]
    extras = rest[n_w:n_w + n_extra]
    i = pl.program_id(1)
    if cast:
        outs = rest[n_w + n_extra:len(rest) - n_w]
        wbs = rest[len(rest) - n_w:]
        changed = jnp.logical_or(i == 0, gid_ref[i] != gid_ref[jnp.maximum(i - 1, 0)])

        @pl.when(changed)
        def _():
            for w_ref, wb in zip(w_refs, wbs):
                _cast_panel(w_ref, wb, kc)
    else:
        outs = rest[n_w + n_extra:]
        wbs = w_refs

    @pl.when(i < nv_ref[0])
    def _():
        x = x_ref[...]
        accs = [jnp.dot(x, wb[...], preferred_element_type=F32) for wb in wbs]
        epi(accs, extras, outs)

    @pl.when(i >= nv_ref[0])
    def _():
        for o in outs:
            o[...] = jnp.zeros(o.shape, o.dtype)


def _epi_store(accs, extras, outs):
    outs[0][...] = accs[0].astype(outs[0].dtype)


def _epi_swiglu(accs, extras, outs):
    outs[0][...] = (_silu(accs[0]) * accs[1]).astype(outs[0].dtype)


def _epi_rowscale(accs, extras, outs):
    outs[0][...] = accs[0] * extras[0][...]


def _epi_ple(accs, extras, outs):
    xres, p_ref, wp_ref = extras
    pp = jnp.dot(p_ref[...], wp_ref[...].astype(BF16), preferred_element_type=F32)
    y = xres[...] + pp * _sigmoid(accs[0])
    outs[0][...] = y
    outs[1][...] = y.astype(BF16)


def _matmul(x, ws, gbase, *, name, tn, tm, epi, out_dtypes, extras=(), gid=None, n_valid=None,
            single_buffer_w=False):
    M, K = x.shape
    N = ws[0].shape[2]
    assert all(w.shape[1] == K for w in ws)
    assert M % tm == 0 and N % tn == 0
    n_i = M // tm
    if gid is None:
        gid = jnp.zeros((n_i,), jnp.int32)
        n_valid = jnp.full((1,), n_i, jnp.int32)
    cast = ws[0].dtype != BF16
    kc = _div_tile(K, 1024, BF16_SUBLANES)
    w_mode = dict(pipeline_mode=pl.Buffered(1)) if single_buffer_w else {}
    in_specs = [pl.BlockSpec((tm, K), lambda j, i, g, nv: (i, 0))]
    for _ in ws:
        in_specs.append(pl.BlockSpec((None, K, tn), lambda j, i, g, nv: (gbase + g[i], 0, j), **w_mode))
    for _, bs, im in extras:
        in_specs.append(pl.BlockSpec(bs, im))
    out_shape = [jax.ShapeDtypeStruct((M, N), dt) for dt in out_dtypes]
    out_specs = [pl.BlockSpec((tm, tn), lambda j, i, g, nv: (i, j)) for _ in out_dtypes]
    kern = functools.partial(_mm_kernel, n_w=len(ws), n_extra=len(extras), epi=epi, kc=kc, cast=cast)
    res = pl.pallas_call(
        kern,
        grid_spec=pltpu.PrefetchScalarGridSpec(
            num_scalar_prefetch=2, grid=(N // tn, n_i), in_specs=in_specs, out_specs=out_specs,
            scratch_shapes=[pltpu.VMEM((K, tn), BF16) for _ in ws] if cast else []),
        out_shape=out_shape, name=name,
        compiler_params=_cparams("arbitrary", "arbitrary"),
    )(gid, n_valid, x, *ws, *[e[0] for e in extras])
    return res


def _post_norm_kernel(x_ref, s_ref, g_ref, b_ref, o_ref, ob_ref):
    y = _layer_norm(ALPHA * x_ref[...] + s_ref[...], g_ref[...], b_ref[...])
    o_ref[...] = y
    ob_ref[...] = y.astype(BF16)


def _post_norm(x, sub, g, b, layer):
    M, D = x.shape
    tm = _div_tile(M, 256, BF16_SUBLANES)
    g3 = g.reshape(g.shape[0], 1, D)
    b3 = b.reshape(b.shape[0], 1, D)
    row = pl.BlockSpec((tm, D), lambda i: (i, 0))
    vec = pl.BlockSpec((None, 1, D), lambda i: (layer, 0, 0))
    return pl.pallas_call(
        _post_norm_kernel, grid=(M // tm,), in_specs=[row, row, vec, vec], out_specs=[row, row],
        out_shape=[jax.ShapeDtypeStruct((M, D), F32), jax.ShapeDtypeStruct((M, D), BF16)],
        name="post_norm", compiler_params=_cparams("arbitrary"),
    )(x, sub, g3, b3)


def _norm_silu_kernel(c_ref, g_ref, b_ref, o_ref):
    o_ref[...] = _silu(_layer_norm(c_ref[...], g_ref[...], b_ref[...])).astype(BF16)


def _norm_silu(c, g, b, layer):
    M, D = c.shape
    tm = _div_tile(M, 512, BF16_SUBLANES)
    g3 = g.reshape(g.shape[0], 1, D)
    b3 = b.reshape(b.shape[0], 1, D)
    row = pl.BlockSpec((tm, D), lambda i: (i, 0))
    vec = pl.BlockSpec((None, 1, D), lambda i: (layer, 0, 0))
    return pl.pallas_call(
        _norm_silu_kernel, grid=(M // tm,), in_specs=[row, vec, vec], out_specs=row,
        out_shape=jax.ShapeDtypeStruct((M, D), BF16), name="conv_norm_silu",
        compiler_params=_cparams("arbitrary"),
    )(c, g3, b3)


def _retention_kernel(q_ref, k_ref, v_ref, g_ref, cos_ref, sin_ref, dec_ref, qs_ref, ks_ref, rs_ref,
                      gg_ref, gb_ref, r0_ref, o_ref, st_ref, r_scr, *, n_chunks):
    c = pl.program_id(2)

    @pl.when(c == 0)
    def _():
        r_scr[...] = r0_ref[...]

    cos = cos_ref[...]
    sin = sin_ref[...]
    half = DK_RET // 2

    def rot(x):
        x1 = x[:, :half]
        x2 = x[:, half:]
        return jnp.concatenate([x1 * cos - x2 * sin, x1 * sin + x2 * cos], axis=-1)

    for j in range(RET_HEADS_PER_STEP):
        cols = slice(j * DK_RET, (j + 1) * DK_RET)
        qr = rot(q_ref[:, cols])
        kr = rot(k_ref[:, cols]) * (DK_RET ** -0.5)
        qb = qr.astype(BF16)
        kb = kr.astype(BF16)
        vb = v_ref[:, cols].astype(BF16)
        s = lax.dot_general(qb, kb, (((1,), (1,)), ((), ())), preferred_element_type=F32) * dec_ref[j]
        inner = jnp.dot(s.astype(BF16), vb, preferred_element_type=F32)
        r = r_scr[j]
        cross = jnp.dot(qb, r.astype(BF16), preferred_element_type=F32) * qs_ref[j]
        ksc = (kr * ks_ref[j]).astype(BF16)
        r_new = r * rs_ref[j] + lax.dot_general(ksc, vb, (((0,), (0,)), ((), ())), preferred_element_type=F32)
        r_scr[j] = r_new
        y = _layer_norm(inner + cross, gg_ref[:, cols], gb_ref[:, cols])
        o_ref[:, cols] = (_silu(g_ref[:, cols]) * y).astype(o_ref.dtype)

    @pl.when(c == n_chunks - 1)
    def _():
        st_ref[...] = r_scr[...]


def _retention(h, row_blk0, n_batch, n_chunks, cos, sin, decay, qsc, ksc, rsc, gn_g, gn_b, r0, layer):
    C = RET_CHUNK
    nh = RET_HEADS_PER_STEP
    hq = nh * DK_RET
    ncb = RET_W // hq

    def rows(col0):
        return pl.BlockSpec((C, hq), lambda b, hh, c: (row_blk0 + b * n_chunks + c, col0 + hh))

    tab = pl.BlockSpec((C, DK_RET // 2), lambda b, hh, c: (c, 0))
    per_head = lambda shp: pl.BlockSpec((nh,) + shp, lambda b, hh, c: (hh, 0, 0))
    gvec = pl.BlockSpec((None, 1, hq), lambda b, hh, c: (layer, 0, hh))
    state = pl.BlockSpec((None, nh, DK_RET, DV_RET), lambda b, hh, c: (b, hh, 0, 0))
    out_rows = n_batch * n_chunks * C
    return pl.pallas_call(
        functools.partial(_retention_kernel, n_chunks=n_chunks),
        grid=(n_batch, H_RET // nh, n_chunks),
        in_specs=[rows(0), rows(ncb), rows(2 * ncb), rows(3 * ncb), tab, tab,
                  per_head((C, C)), per_head((C, 1)), per_head((C, 1)), per_head((1, DV_RET)),
                  gvec, gvec, state],
        out_specs=[pl.BlockSpec((C, hq), lambda b, hh, c: (b * n_chunks + c, hh)), state],
        out_shape=[jax.ShapeDtypeStruct((out_rows, RET_W), BF16),
                   jax.ShapeDtypeStruct((n_batch, H_RET, DK_RET, DV_RET), F32)],
        scratch_shapes=[pltpu.VMEM((nh, DK_RET, DV_RET), F32)], name="retention",
        compiler_params=_cparams("arbitrary", "arbitrary", "arbitrary"),
    )(h, h, h, h, cos, sin, decay, qsc, ksc, rsc,
      gn_g.reshape(gn_g.shape[0], 1, RET_W), gn_b.reshape(gn_b.shape[0], 1, RET_W), r0)


def _retention_tables(chunk, pad_to):
    log_gamma = jnp.log1p(-jnp.exp2(-5.0 - jnp.arange(H_RET, dtype=F32)))
    idx = jnp.arange(chunk, dtype=F32)
    lg = log_gamma[:, None]
    diff = idx[:, None] - idx[None, :]
    decay = jnp.where(diff >= 0, jnp.exp(lg[:, :, None] * jnp.maximum(diff, 0.0)), 0.0)
    qsc = jnp.exp(lg * (idx + 1.0))[:, :, None]
    ksc = jnp.exp(lg * (chunk - 1.0 - idx))[:, :, None]
    rsc = jnp.broadcast_to(jnp.exp(log_gamma * chunk)[:, None, None], (H_RET, 1, DV_RET))
    p = pad_to - chunk
    decay = jnp.pad(decay, ((0, 0), (0, p), (0, p)))
    qsc = jnp.pad(qsc, ((0, 0), (0, p), (0, 0)))
    ksc = jnp.pad(ksc, ((0, 0), (0, p), (0, 0)))
    return decay, qsc, ksc, rsc


def _rope_tables(pos, pad_to):
    half = DK_RET // 2
    inv = jnp.exp(-math.log(ROPE_BASE) * jnp.arange(half, dtype=F32) / half)
    ang = pos.astype(F32)[:, None] * inv[None, :]
    p = pad_to - pos.shape[0]
    return jnp.pad(jnp.cos(ang), ((0, p), (0, 0))), jnp.pad(jnp.sin(ang), ((0, p), (0, 0)))


CONV_HIST = 32
CONV_LANES = 512
CONV_ROWS = 64


def _conv_kernel(a_ref, b_ref, st_ref, w_ref, bias_ref, o_ref, buf_ref, full, *, tt, n_t):
    t = pl.program_id(2)
    kh = CONV_K - 1
    off = CONV_HIST - kh

    @pl.when(t == 0)
    def _():
        full[0:off, :] = jnp.zeros((off, CONV_LANES), F32)
        full[off:CONV_HIST, :] = st_ref[...]

    @pl.when(t > 0)
    def _():
        full[0:CONV_HIST, :] = full[tt:tt + CONV_HIST, :]

    full[CONV_HIST:CONV_HIST + tt, :] = a_ref[...] * _sigmoid(b_ref[...])
    rc = min(CONV_ROWS, tt)
    for r0 in range(0, tt, rc):
        acc = jnp.zeros((rc, CONV_LANES), F32) + bias_ref[...]
        for j in range(CONV_K):
            acc = acc + w_ref[j:j + 1, :] * full[off + r0 + j:off + r0 + j + rc, :]
        o_ref[r0:r0 + rc, :] = acc

    @pl.when(t == n_t - 1)
    def _():
        buf_ref[...] = full[CONV_HIST + tt - kh:CONV_HIST + tt, :]


def _conv(h, row_blk0, n_batch, n_t, tt, col_a, col_b, state, w, bias, layer):
    W = w.shape[2]
    n_l = W // CONV_LANES
    kh = CONV_K - 1
    rows = lambda col0: pl.BlockSpec((tt, CONV_LANES), lambda b, l, t: (row_blk0 + b * n_t + t, col0 + l))
    return pl.pallas_call(
        functools.partial(_conv_kernel, tt=tt, n_t=n_t),
        grid=(n_batch, n_l, n_t),
        in_specs=[rows(col_a), rows(col_b),
                  pl.BlockSpec((None, kh, CONV_LANES), lambda b, l, t: (b, 0, l)),
                  pl.BlockSpec((None, CONV_K, CONV_LANES), lambda b, l, t: (layer, 0, l)),
                  pl.BlockSpec((None, 1, CONV_LANES), lambda b, l, t: (layer, 0, l))],
        out_specs=[pl.BlockSpec((tt, CONV_LANES), lambda b, l, t: (b * n_t + t, l)),
                   pl.BlockSpec((None, kh, CONV_LANES), lambda b, l, t: (b, 0, l))],
        out_shape=[jax.ShapeDtypeStruct((n_batch * n_t * tt, W), F32),
                   jax.ShapeDtypeStruct((n_batch, kh, W), F32)],
        scratch_shapes=[pltpu.VMEM((CONV_HIST + tt, CONV_LANES), F32)], name="glu_conv",
        compiler_params=_cparams("arbitrary", "arbitrary", "arbitrary"),
    )(h, h, state, w, bias.reshape(bias.shape[0], 1, W))


def _sb_block(z, mask, carry, tri):
    w = jnp.maximum(z, 0.0) + jnp.log(1.0 + jnp.exp(-jnp.abs(z)))
    if mask is not None:
        w = jnp.where(mask, w, 0.0)
    w_hi = w.astype(BF16)
    w_lo = (w - w_hi.astype(F32)).astype(BF16)
    tail = jnp.dot(w_hi, tri, preferred_element_type=F32) + jnp.dot(w_lo, tri, preferred_element_type=F32)
    a = jnp.exp(z - tail - carry)
    if mask is not None:
        a = jnp.where(mask, a, 0.0)
    return a, carry + tail[:, :1]


def _tri(n):
    r = jnp.arange(n)
    return (r[:, None] >= r[None, :]).astype(BF16)


def _sb_prompt_kernel(bias_ref, q_ref, k_ref, v_ref, tri_ref, o_ref, *, tq):
    nh = SB_HEADS_PER_STEP
    hp = pl.program_id(1)
    qi = pl.program_id(2)
    scale = D_SB ** -0.5
    tri = tri_ref[...]
    cols = [slice(g * D_SB, (g + 1) * D_SB) for g in range(nh)]
    qbs = [q_ref[:, cols[g]].astype(BF16) for g in range(nh)]
    biases = [bias_ref[hp * nh + g] for g in range(nh)]

    def key_block(kb):
        return pl.multiple_of(jnp.maximum(kb, 0) * tq, tq)

    def logits(r0):
        return tuple(lax.dot_general(qbs[g], k_ref[pl.ds(r0, tq), cols[g]].astype(BF16), (((1,), (1,)), ((), ())),
                                     preferred_element_type=F32) for g in range(nh))

    def visit(r0, zs, mask, state):
        new = []
        for g in range(nh):
            carry, o = state[g]
            vb = v_ref[pl.ds(r0, tq), cols[g]].astype(BF16)
            a, carry = _sb_block(zs[g] * scale + biases[g], mask, carry, tri)
            new.append((carry, o + jnp.dot(a.astype(BF16), vb, preferred_element_type=F32)))
        return tuple(new)

    row = lax.broadcasted_iota(jnp.int32, (tq, tq), 0)
    col = lax.broadcasted_iota(jnp.int32, (tq, tq), 1)
    state = tuple((jnp.zeros((tq, 1), F32), jnp.zeros((tq, D_SB), F32)) for _ in range(nh))
    zs_next = logits(key_block(qi - 1))
    state = visit(key_block(qi), logits(key_block(qi)), col < row, state)

    def body(kk, st):
        zs, state = st
        return logits(key_block(qi - kk - 1)), visit(key_block(qi - kk), zs, None, state)

    _, state = lax.fori_loop(1, qi + 1, body, (zs_next, state))
    for g in range(nh):
        o_ref[:, cols[g]] = state[g][1].astype(BF16)


def _sb_prompt(h, bias, n_batch, seq):
    tq = min(SB_Q_TILE, seq)
    nq = seq // tq
    wide = SB_HEADS_PER_STEP * D_SB
    n_hp = H_SB // SB_HEADS_PER_STEP
    return pl.pallas_call(
        functools.partial(_sb_prompt_kernel, tq=tq),
        grid_spec=pltpu.PrefetchScalarGridSpec(
            num_scalar_prefetch=1, grid=(n_batch, n_hp, nq),
            in_specs=[pl.BlockSpec((tq, wide), lambda b, hp, qi, bias: (b * nq + qi, hp)),
                      pl.BlockSpec((seq, wide), lambda b, hp, qi, bias: (b, n_hp + hp)),
                      pl.BlockSpec((seq, wide), lambda b, hp, qi, bias: (b, 2 * n_hp + hp)),
                      pl.BlockSpec((tq, tq), lambda b, hp, qi, bias: (0, 0))],
            out_specs=pl.BlockSpec((tq, wide), lambda b, hp, qi, bias: (b * nq + qi, hp))),
        out_shape=jax.ShapeDtypeStruct((n_batch * seq, SB_W), BF16), name="sb_prompt",
        compiler_params=_cparams("arbitrary", "arbitrary", "arbitrary"),
    )(bias, h, h, h, _tri(tq))


def _sb_sample_kernel(pt_ref, bias_ref, q_ref, kn_ref, vn_ref, *rest, n_steps, t_dec, page):
    P = SB_PAGES_PER_STEP
    HG = SB_HEAD_GROUP
    k_pages = rest[:P]
    v_pages = rest[P:2 * P]
    tri_ref, o_ref, qs, carry, acc = rest[2 * P:]
    hg = pl.program_id(1)
    s = pl.program_id(2)
    scale = D_SB ** -0.5
    QP = SB_Q_PAD
    rows = HG * QP
    lanes = lambda j: slice(j * D_SB, (j + 1) * D_SB)

    def bias_col():
        hrow = lax.broadcasted_iota(jnp.int32, (rows, 1), 0) // QP
        out = jnp.zeros((rows, 1), F32)
        for j in range(HG):
            out = jnp.where(hrow == j, bias_ref[hg * HG + j], out)
        return out

    def process(kcat, vcat, mask, tri):
        z = jnp.concatenate(
            [lax.dot_general(qs[j], kcat[j], (((1,), (1,)), ((), ())), preferred_element_type=F32)
             for j in range(HG)], axis=0)
        a, new_carry = _sb_block(z * scale + bias_col(), mask, carry[...], tri)
        carry[...] = new_carry
        ab = a.astype(BF16)
        for j in range(HG):
            acc[j * QP:(j + 1) * QP, :] += jnp.dot(ab[j * QP:(j + 1) * QP, :], vcat[j], preferred_element_type=F32)

    @pl.when(s == 0)
    def _():
        carry[...] = jnp.zeros_like(carry)
        acc[...] = jnp.zeros_like(acc)
        pad = jnp.zeros((QP - t_dec, D_SB), F32)
        for j in range(HG):
            qs[j] = jnp.concatenate([q_ref[:, lanes(j)], pad], axis=0).astype(BF16)
        kpad = jnp.zeros((page - t_dec, D_SB), F32)
        kcat = [jnp.concatenate([kn_ref[:, lanes(j)], kpad], axis=0).astype(BF16) for j in range(HG)]
        vcat = [jnp.concatenate([vn_ref[:, lanes(j)], kpad], axis=0).astype(BF16) for j in range(HG)]
        trow = lax.broadcasted_iota(jnp.int32, (rows, page), 0) % QP
        col = lax.broadcasted_iota(jnp.int32, (rows, page), 1)
        process(kcat, vcat, col < trow, tri_ref[0:page, 0:page])

    @pl.when(s > 0)
    def _():
        kts = [pltpu.einshape("khd->hkd", kp[...]) for kp in k_pages]
        vts = [pltpu.einshape("khd->hkd", vp[...]) for vp in v_pages]
        kcat = [jnp.concatenate([kt[j] for kt in kts], axis=0).astype(BF16) for j in range(HG)]
        vcat = [jnp.concatenate([vt[j] for vt in vts], axis=0).astype(BF16) for j in range(HG)]
        process(kcat, vcat, None, tri_ref[...])

    @pl.when(s == n_steps - 1)
    def _():
        for j in range(HG):
            o_ref[:, lanes(j)] = acc[j * QP:j * QP + t_dec, :]


def _sb_sample(h, row_blk0, cache_k, cache_v, page_table, bias, layer, n_batch, t_dec):
    P = SB_PAGES_PER_STEP
    HG = SB_HEAD_GROUP
    n_pages = page_table.shape[1]
    page = cache_k.shape[2]
    assert n_pages % P == 0 and t_dec <= SB_Q_PAD and t_dec <= page and H_SB % HG == 0
    n_groups = n_pages // P
    n_steps = n_groups + 1
    n_hg = H_SB // HG
    wide = HG * D_SB

    def page_spec(i):
        def im(b, hg, s, pt, bias):
            g = n_groups - jnp.maximum(s, 1)
            return (layer, pt[b * n_pages + g * P + i], 0, hg, 0)
        return pl.BlockSpec((None, None, page, HG, D_SB), im)

    new = lambda part: pl.BlockSpec((t_dec, wide), lambda b, hg, s, pt, bias: (row_blk0 + b, part * n_hg + hg))
    return pl.pallas_call(
        functools.partial(_sb_sample_kernel, n_steps=n_steps, t_dec=t_dec, page=page),
        grid_spec=pltpu.PrefetchScalarGridSpec(
            num_scalar_prefetch=2, grid=(n_batch, n_hg, n_steps),
            in_specs=[new(0), new(1), new(2)] + [page_spec(i) for i in range(P)] * 2
            + [pl.BlockSpec((P * page, P * page), lambda b, hg, s, pt, bias: (0, 0))],
            out_specs=pl.BlockSpec((t_dec, wide), lambda b, hg, s, pt, bias: (b, hg)),
            scratch_shapes=[pltpu.VMEM((HG, SB_Q_PAD, D_SB), BF16),
                            pltpu.VMEM((HG * SB_Q_PAD, 1), F32),
                            pltpu.VMEM((HG * SB_Q_PAD, D_SB), F32)]),
        out_shape=jax.ShapeDtypeStruct((n_batch * t_dec, SB_W), F32), name="sb_sample",
        compiler_params=_cparams("arbitrary", "arbitrary", "arbitrary"),
    )(page_table.reshape(-1), bias, h, h, h, *([cache_k] * P), *([cache_v] * P), _tri(P * page))


def _split_bf16(x):
    hi = x.astype(BF16)
    return hi, (x - hi.astype(F32)).astype(BF16)


def _router_kernel(x_ref, w_ref, g_ref, e_ref, *, n_exp):
    xh, xl = _split_bf16(x_ref[...])
    wh, wl = _split_bf16(w_ref[...])
    dot = functools.partial(jnp.dot, preferred_element_type=F32)
    logits = dot(xh, wh) + dot(xl, wh) + dot(xh, wl)
    lane = lax.broadcasted_iota(jnp.int32, logits.shape, 1)
    lanef = lane.astype(F32)
    neg = jnp.float32(-jnp.inf)
    logits = jnp.where(lane < n_exp, logits, neg)
    m1 = jnp.max(logits, axis=-1, keepdims=True)
    i1 = jnp.min(jnp.where(logits == m1, lanef, float(LANES)), axis=-1, keepdims=True)
    rest = jnp.where(lanef == i1, neg, logits)
    m2 = jnp.max(rest, axis=-1, keepdims=True)
    i2 = jnp.min(jnp.where(rest == m2, lanef, float(LANES)), axis=-1, keepdims=True)
    e2 = jnp.exp(m2 - m1)
    den = 1.0 + e2
    g_ref[...] = jnp.where(lane == 0, 1.0 / den, jnp.where(lane == 1, e2 / den, 0.0))
    e_ref[...] = jnp.where(lane == 0, i1, jnp.where(lane == 1, i2, 0.0)).astype(jnp.int32)


def _router(x, router_w, layer):
    M, D = x.shape
    n_exp = router_w.shape[2]
    tm = _div_tile(M, 256, 8)
    wpad = jnp.pad(router_w, ((0, 0), (0, 0), (0, LANES - n_exp)))
    row = lambda w: pl.BlockSpec((tm, w), lambda i: (i, 0))
    gates, eidx = pl.pallas_call(
        functools.partial(_router_kernel, n_exp=n_exp), grid=(M // tm,),
        in_specs=[row(D), pl.BlockSpec((None, D, LANES), lambda i: (layer, 0, 0))],
        out_specs=[row(LANES), row(LANES)],
        out_shape=[jax.ShapeDtypeStruct((M, LANES), F32), jax.ShapeDtypeStruct((M, LANES), jnp.int32)],
        name="moe_router", compiler_params=_cparams("arbitrary"),
    )(x, wpad)
    return gates[:, :TOP_K], eidx[:, :TOP_K]


def _row_copy(src_hbm, dst, src_row, dst_row, sem):
    return pltpu.make_async_copy(src_hbm.at[pl.ds(src_row, 1), :], dst.at[pl.ds(dst_row, 1), :], sem)


def _gather_kernel(tok_ref, x_hbm, o_ref, buf, sem, *, tg):
    i = pl.program_id(0)

    def issue(r, carry):
        _row_copy(x_hbm, buf, tok_ref[i * tg + r], r, sem).start()
        return carry

    def wait(r, carry):
        _row_copy(x_hbm, buf, 0, r, sem).wait()
        return carry

    lax.fori_loop(0, tg, issue, 0)
    lax.fori_loop(0, tg, wait, 0)
    o_ref[...] = buf[...].astype(BF16)


def _gather_rows(x, row_token, tg):
    R = row_token.shape[0]
    D = x.shape[1]
    return pl.pallas_call(
        functools.partial(_gather_kernel, tg=tg),
        grid_spec=pltpu.PrefetchScalarGridSpec(
            num_scalar_prefetch=1, grid=(R // tg,),
            in_specs=[pl.BlockSpec(memory_space=pl.ANY)],
            out_specs=pl.BlockSpec((tg, D), lambda i, tok: (i, 0)),
            scratch_shapes=[pltpu.VMEM((tg, D), F32), pltpu.SemaphoreType.DMA(())]),
        out_shape=jax.ShapeDtypeStruct((R, D), BF16), name="moe_gather",
        compiler_params=_cparams("arbitrary"),
    )(row_token, x)


def _combine_norm_kernel(dest_ref, x_ref, g_ref, b_ref, y_hbm, o_ref, ob_ref, buf, sem, *, tc):
    i = pl.program_id(0)

    def issue(r, carry):
        for k in range(TOP_K):
            _row_copy(y_hbm, buf.at[k], dest_ref[(i * tc + r) * TOP_K + k], r, sem).start()
        return carry

    def wait(r, carry):
        for k in range(TOP_K):
            _row_copy(y_hbm, buf.at[k], 0, r, sem).wait()
        return carry

    lax.fori_loop(0, tc, issue, 0)
    lax.fori_loop(0, tc, wait, 0)
    moe = buf[0] + buf[1]
    y = _layer_norm(ALPHA * x_ref[...] + moe, g_ref[...], b_ref[...])
    o_ref[...] = y
    ob_ref[...] = y.astype(BF16)


def _combine_norm(x, y_rows, dest, g, b, layer):
    M, D = x.shape
    tc = _div_tile(M, MOE_COMBINE_TILE, BF16_SUBLANES)
    g3 = g.reshape(g.shape[0], 1, D)
    b3 = b.reshape(b.shape[0], 1, D)
    row = pl.BlockSpec((tc, D), lambda i, d: (i, 0))
    vec = pl.BlockSpec((None, 1, D), lambda i, d: (layer, 0, 0))
    return pl.pallas_call(
        functools.partial(_combine_norm_kernel, tc=tc),
        grid_spec=pltpu.PrefetchScalarGridSpec(
            num_scalar_prefetch=1, grid=(M // tc,),
            in_specs=[row, vec, vec, pl.BlockSpec(memory_space=pl.ANY)],
            out_specs=[row, row],
            scratch_shapes=[pltpu.VMEM((TOP_K, tc, D), F32), pltpu.SemaphoreType.DMA(())]),
        out_shape=[jax.ShapeDtypeStruct((M, D), F32), jax.ShapeDtypeStruct((M, D), BF16)],
        name="moe_combine_norm", compiler_params=_cparams("arbitrary"),
    )(dest.reshape(-1), x, g3, b3, y_rows)


def _moe_plan(eidx, gates, n_exp, tile):
    M = eidx.shape[0]
    n_tiles = (M * TOP_K + n_exp * (tile - 1)) // tile
    R = n_tiles * tile
    onehot = (eidx[:, :, None] == jnp.arange(n_exp, dtype=jnp.int32)[None, None, :]).astype(jnp.int32)
    per_tok = onehot.sum(axis=1)
    counts = per_tok.sum(axis=0)
    pos = jnp.cumsum(per_tok, axis=0) - per_tok
    padded = ((counts + tile - 1) // tile) * tile
    ends = jnp.cumsum(padded)
    offs = ends - padded
    dest = jnp.take_along_axis(offs[None, :] + pos, eidx, axis=1)
    tok = jnp.broadcast_to(jnp.arange(M, dtype=jnp.int32)[:, None], (M, TOP_K))
    row_token = jnp.zeros((R,), jnp.int32).at[dest.reshape(-1)].set(tok.reshape(-1))
    row_gate = jnp.zeros((R,), F32).at[dest.reshape(-1)].set(gates.reshape(-1))
    tile_start = jnp.arange(n_tiles, dtype=jnp.int32) * tile
    tile_exp = jnp.minimum(jnp.searchsorted(ends, tile_start, side="right"), n_exp - 1).astype(jnp.int32)
    n_valid = (ends[-1] // tile).astype(jnp.int32).reshape(1)
    last = tile_exp[jnp.maximum(n_valid[0] - 1, 0)]
    tile_exp = jnp.where(tile_start < ends[-1], tile_exp, last)
    return row_token, row_gate.reshape(R, 1), dest.astype(jnp.int32), tile_exp, n_valid


def _moe_norm(x, router_w, w1, w3, w2, g, b, layer, moe_layer):
    n_exp = router_w.shape[2]
    gates, eidx = _router(x, router_w, moe_layer)
    row_token, row_gate, dest, tile_exp, n_valid = _moe_plan(eidx, gates, n_exp, MOE_ROW_TILE)
    xs = _gather_rows(x, row_token, MOE_GATHER_TILE)
    w1f = w1.reshape((-1,) + w1.shape[2:])
    w3f = w3.reshape((-1,) + w3.shape[2:])
    w2f = w2.reshape((-1,) + w2.shape[2:])
    gb = moe_layer * n_exp
    (hs,) = _matmul(xs, [w1f, w3f], gb, name="moe_up", tn=512, tm=MOE_ROW_TILE, epi=_epi_swiglu, out_dtypes=[BF16],
                    gid=tile_exp, n_valid=n_valid)
    gate_spec = (row_gate, (MOE_ROW_TILE, 1), lambda j, i, gid, nv: (i, 0))
    (ys,) = _matmul(hs, [w2f], gb, name="moe_down", tn=512, tm=MOE_ROW_TILE, epi=_epi_rowscale, out_dtypes=[F32],
                    extras=[gate_spec], gid=tile_exp, n_valid=n_valid)
    return _combine_norm(x, ys, dest, g, b, layer)


def _ple(x, xb, pb, ple_proj, ple_gate, layer, tm):
    tn = 512
    extras = [(x, (tm, tn), lambda j, i, g, nv: (i, j)),
              (pb, (tm, pb.shape[1]), lambda j, i, g, nv: (i, 0)),
              (ple_proj, (None, ple_proj.shape[1], tn), lambda j, i, g, nv: (layer, 0, j))]
    return _matmul(xb, [ple_gate], layer, name="ple", tn=tn, tm=tm, epi=_epi_ple, out_dtypes=[F32, BF16],
                   extras=extras, single_buffer_w=True)


def kernel(x_prompt, x_sample, p_prompt, p_sample, state_ret, state_conv, cache_k, cache_v, page_table,
           ln_mix_g, ln_mix_b, ln_ffn_g, ln_ffn_b, ple_proj, ple_gate, ev_w_in, ev_w_out, ret_gn_g, ret_gn_b,
           conv_w, conv_b, conv_ln_g, conv_ln_b, ffn_w1, ffn_w3, ffn_w2, sb_w_qkv, sb_w_out, sb_bias,
           moe_router, moe_w1, moe_w3, moe_w2):
    B, T, D = x_prompt.shape
    Bs, Ts, _ = x_sample.shape
    Mp, Ms = B * T, Bs * Ts
    M = Mp + Ms
    past = page_table.shape[1] * cache_k.shape[2]
    conv_width = conv_w.shape[2]
    assert T % RET_CHUNK == 0 and Ts <= RET_CHUNK and Mp % RET_CHUNK == 0 and Mp % Ts == 0
    assert conv_width % CONV_LANES == 0 and (4 * RET_W) % CONV_LANES == 0

    tm = _div_tile(M, 1376, BF16_SUBLANES)
    tm_n = _div_tile(M, 688, BF16_SUBLANES)
    ffn_w2_bf16 = ffn_w2.astype(BF16)
    x = jnp.concatenate([x_prompt.reshape(Mp, D), x_sample.reshape(Ms, D)], axis=0)
    xb = x.astype(BF16)
    pb = jnp.concatenate([p_prompt.reshape(DEPTH, Mp, -1), p_sample.reshape(DEPTH, Ms, -1)], axis=1).astype(BF16)

    nc = T // RET_CHUNK
    cos_p, sin_p = _rope_tables(jnp.arange(T, dtype=jnp.int32), T)
    cos_s, sin_s = _rope_tables(past + jnp.arange(Ts, dtype=jnp.int32), RET_CHUNK)
    tab_p = _retention_tables(RET_CHUNK, RET_CHUNK)
    tab_s = _retention_tables(Ts, RET_CHUNK)
    zero_state = jnp.zeros((B, H_RET, DK_RET, DV_RET), F32)
    zero_buf = jnp.zeros((B, CONV_K - 1, conv_width), F32)
    conv_tt = min(128, T)

    ret_p, ret_s, conv_p, conv_s, kp, vp, ks, vs = [], [], [], [], [], [], [], []
    for i in range(DEPTH):
        if i % 2 == 0:
            e = i // 2
            (h,) = _matmul(xb, [ev_w_in], e, name="ev_in", tn=512, tm=tm, epi=_epi_store, out_dtypes=[F32])
            r_p, st_p = _retention(h, 0, B, nc, cos_p, sin_p, *tab_p, ret_gn_g, ret_gn_b, zero_state, e)
            hs_pad = jnp.pad(h[Mp:, :4 * RET_W].reshape(Bs, Ts, 4 * RET_W),
                             ((0, 0), (0, RET_CHUNK - Ts), (0, 0))).reshape(Bs * RET_CHUNK, 4 * RET_W)
            r_s, st_s = _retention(hs_pad, 0, Bs, 1, cos_s, sin_s, *tab_s, ret_gn_g, ret_gn_b, state_ret[e], e)
            r_s = r_s.reshape(Bs, RET_CHUNK, RET_W)[:, :Ts].reshape(Ms, RET_W)
            ca = (4 * RET_W) // CONV_LANES
            cb = ca + conv_width // CONV_LANES
            c_p, buf_p = _conv(h, 0, B, T // conv_tt, conv_tt, ca, cb, zero_buf, conv_w, conv_b, e)
            c_s, buf_s = _conv(h, Mp // Ts, Bs, 1, Ts, ca, cb, state_conv[e], conv_w, conv_b, e)
            cact = _norm_silu(jnp.concatenate([c_p, c_s], axis=0), conv_ln_g, conv_ln_b, e)
            mix = jnp.concatenate([jnp.concatenate([r_p, r_s], axis=0), cact], axis=1)
            (mo,) = _matmul(mix, [ev_w_out], e, name="ev_out", tn=512, tm=tm, epi=_epi_store, out_dtypes=[F32])
            x, xb = _post_norm(x, mo, ln_mix_g, ln_mix_b, i)
            (hf,) = _matmul(xb, [ffn_w1, ffn_w3], e, name="ffn_up", tn=256, tm=tm, epi=_epi_swiglu,
                            out_dtypes=[BF16])
            (f2,) = _matmul(hf, [ffn_w2_bf16], e, name="ffn_down", tn=512, tm=tm_n, epi=_epi_store, out_dtypes=[F32],
                            single_buffer_w=True)
            x, xb = _post_norm(x, f2, ln_ffn_g, ln_ffn_b, i)
            ret_p.append(st_p)
            ret_s.append(st_s)
            conv_p.append(buf_p)
            conv_s.append(buf_s)
        else:
            o = i // 2
            (h,) = _matmul(xb, [sb_w_qkv], o, name="sb_qkv", tn=512, tm=tm, epi=_epi_store, out_dtypes=[F32])
            a_p = _sb_prompt(h, sb_bias[o], B, T)
            a_s = _sb_sample(h, Mp // Ts, cache_k, cache_v, page_table, sb_bias[o], o, Bs, Ts)
            att = jnp.concatenate([a_p, a_s.astype(BF16)], axis=0)
            (mo,) = _matmul(att, [sb_w_out], o, name="sb_out", tn=512, tm=tm, epi=_epi_store, out_dtypes=[F32])
            x, xb = _post_norm(x, mo, ln_mix_g, ln_mix_b, i)
            x, xb = _moe_norm(x, moe_router, moe_w1, moe_w3, moe_w2, ln_ffn_g, ln_ffn_b, i, o)
            kp.append(h[:Mp, SB_W:2 * SB_W].reshape(B, T, H_SB, D_SB))
            vp.append(h[:Mp, 2 * SB_W:].reshape(B, T, H_SB, D_SB))
            ks.append(h[Mp:, SB_W:2 * SB_W].reshape(Bs, Ts, H_SB, D_SB))
            vs.append(h[Mp:, 2 * SB_W:].reshape(Bs, Ts, H_SB, D_SB))
        x, xb = _ple(x, xb, pb[i], ple_proj, ple_gate, i, tm_n)

    return (x[:Mp].reshape(B, T, D), x[Mp:].reshape(Bs, Ts, D), jnp.stack(ret_p), jnp.stack(ret_s),
            jnp.stack(conv_p), jnp.stack(conv_s), jnp.stack(kp), jnp.stack(vp), jnp.stack(ks), jnp.stack(vs))
```
